```python
import jax
import jax.numpy as jnp
from jax import lax
import numpy as np

D_MODEL = 1024
BATCH = 4
SEQ = 8192
DEPTH = 2

D_MIX = D_MODEL
GLA_HEADS = 4
GLA_VW = D_MIX // 2
GLA_KW = GLA_VW // 2
GLA_DV = GLA_VW // GLA_HEADS
GLA_DK = GLA_KW // GLA_HEADS
GLA_GATE_RANK = 16
GLA_GATE_NORM = 16.0
GLA_CHUNK = 64
RWKV_VW = D_MIX - GLA_VW
RWKV_HEAD = 64
RWKV_HEADS = RWKV_VW // RWKV_HEAD
DECAY_LORA = 64
AAA_LORA = 64
GATE_LORA = 128
MV_LORA = 32
GN_EPS = 64e-5
D_FF = 128 * ((8 * D_MODEL // 3 + 127) // 128)
CONV_W = 3
NORM_EPS = 1e-6
N_MOD = 6
GLA_COLS = 2 * GLA_KW + 2 * GLA_VW + GLA_GATE_RANK
RWKV_COLS = 3 * RWKV_VW + DECAY_LORA + AAA_LORA + GATE_LORA
IN_COLS = GLA_COLS + RWKV_COLS

kernel_name = 'hybrid_gla_rwkv7_convffn_adaln'


def rms_norm(x, g):
    xf = x.astype(jnp.float32)
    y = xf * lax.rsqrt(jnp.mean(xf * xf, axis=-1, keepdims=True) + NORM_EPS)
    return (y * g.astype(jnp.float32)).astype(x.dtype)


def modulate(x, g, shift, scale):
    return rms_norm(x, g) * (1 + scale[:, None, :]) + shift[:, None, :]


def token_shift(p):
    return jnp.pad(p, ((0, 0), (1, 0), (0, 0)))[:, :-1]


def gla_chunked(q, k, v, gk):
    B, T, H, DK = q.shape
    DV = v.shape[-1]
    NC = T // GLA_CHUNK

    def to_chunks(t):
        return t.reshape(B, NC, GLA_CHUNK, H, t.shape[-1]).transpose(1, 0, 3, 2, 4).astype(jnp.float32)

    qc, kc, vc, gc = (to_chunks(t) for t in (q, k, v, gk))
    G = jnp.cumsum(gc, axis=3)
    causal = jnp.tril(jnp.ones((GLA_CHUNK, GLA_CHUNK), dtype=bool))[:, :, None]

    def step(S, inp):
        qi, ki, vi, Gi = inp
        inter = jnp.einsum('bhid,bhde->bhie', qi * jnp.exp(Gi), S)
        rel = jnp.where(causal, Gi[:, :, :, None, :] - Gi[:, :, None, :, :], -jnp.inf)
        A = jnp.einsum('bhid,bhjd,bhijd->bhij', qi, ki, jnp.exp(rel))
        o = inter + jnp.einsum('bhij,bhje->bhie', A, vi)
        G_last = Gi[:, :, -1, :]
        S = jnp.exp(G_last)[..., None] * S + jnp.einsum(
            'bhjd,bhje->bhde', ki * jnp.exp(G_last[:, :, None, :] - Gi), vi)
        return S, o

    S0 = jnp.zeros((B, H, DK, DV), jnp.float32)
    _, o = lax.scan(step, S0, (qc, kc, vc, G))
    return o.transpose(1, 0, 3, 2, 4).reshape(B, T, H, DV)


def gla_group(p, gk_up, gk_b, norm_g):
    B, T, _ = p.shape
    q = p[..., :GLA_KW].reshape(B, T, GLA_HEADS, GLA_DK) * (GLA_DK ** -0.5)
    k = p[..., GLA_KW:2 * GLA_KW].reshape(B, T, GLA_HEADS, GLA_DK)
    v = p[..., 2 * GLA_KW:2 * GLA_KW + GLA_VW].reshape(B, T, GLA_HEADS, GLA_DV)
    g = p[..., 2 * GLA_KW + GLA_VW:2 * GLA_KW + 2 * GLA_VW]
    z = p[..., 2 * GLA_KW + 2 * GLA_VW:]
    gk = jax.nn.log_sigmoid((z @ gk_up + gk_b).astype(jnp.float32)) / GLA_GATE_NORM
    o = gla_chunked(q, k, v, gk.reshape(B, T, GLA_HEADS, GLA_DK))
    o = rms_norm(o, norm_g).reshape(B, T, GLA_VW)
    return o.astype(p.dtype) * jax.nn.silu(g)


def wkv7_scan(r, w, k, v, a, b):
    B, T, H, N = r.shape
    xs = tuple(t.transpose(1, 0, 2, 3) for t in (r, w, k, v, a, b))

    def step(S, inp):
        rt, wt, kt, vt, at, bt = inp
        sa = jnp.einsum('bhvk,bhk->bhv', S, at)
        S = S * wt[:, :, None, :] + sa[..., None] * bt[:, :, None, :] + vt[..., None] * kt[:, :, None, :]
        return S, jnp.einsum('bhvk,bhk->bhv', S, rt)

    _, y = lax.scan(step, jnp.zeros((B, H, N, N), jnp.float32), xs)
    return y.transpose(1, 0, 2, 3)


def rwkv7_group(p, mu, w_up, w0, a_up, a0, g_up, k_k, k_a, r_k, gn_g, gn_b, v_first, v_up, v0):
    B, T, _ = p.shape
    H, N = RWKV_HEADS, RWKV_HEAD
    p = p + (token_shift(p) - p) * mu
    c0 = 3 * RWKV_VW
    c1 = c0 + DECAY_LORA
    c2 = c1 + AAA_LORA
    c3 = c2 + GATE_LORA
    r = p[..., :RWKV_VW]
    k = p[..., RWKV_VW:2 * RWKV_VW]
    v = p[..., 2 * RWKV_VW:c0]
    w_log = -jax.nn.softplus(-(w0 + jnp.tanh(p[..., c0:c1]) @ w_up)) - 0.5
    decay = jnp.exp(-jnp.exp(w_log.astype(jnp.float32)))
    a = jax.nn.sigmoid(a0 + p[..., c1:c2] @ a_up)
    g = jax.nn.sigmoid(p[..., c2:c3]) @ g_up
    if v_first is None:
        v_first = v
    else:
        v = v + (v_first - v) * jax.nn.sigmoid(v0 + p[..., c3:] @ v_up)
    kk = (k * k_k).reshape(B, T, H, N).astype(jnp.float32)
    kk = kk * lax.rsqrt(jnp.maximum(jnp.sum(kk * kk, axis=-1, keepdims=True), 1e-24))
    k = k * (1 + (a - 1) * k_a)

    def heads(t):
        return t.reshape(B, T, H, N).astype(jnp.float32)

    rh, kh, vh, ah = heads(r), heads(k), heads(v), heads(a)
    y = wkv7_scan(rh, decay.reshape(B, T, H, N), kh, vh, -kk, kk * ah)
    mean = jnp.mean(y, axis=-1, keepdims=True)
    var = jnp.mean(jnp.square(y - mean), axis=-1, keepdims=True)
    y = ((y - mean) * lax.rsqrt(var + GN_EPS)).reshape(B, T, RWKV_VW) * gn_g + gn_b
    bonus = jnp.sum(rh * kh * r_k, axis=-1, keepdims=True) * vh
    y = y + bonus.reshape(B, T, RWKV_VW)
    return (y * g).astype(p.dtype), v_first


def conv_ffn(h, w_up, conv_w, conv_b, w_down):
    T = h.shape[1]
    u = h @ w_up
    up = jnp.pad(u, ((0, 0), (CONV_W - 1, 0), (0, 0)))
    u = sum(conv_w[j] * up[:, j:j + T] for j in range(CONV_W)) + conv_b
    gate, val = jnp.split(u, 2, axis=-1)
    return (jax.nn.silu(gate) * val) @ w_down


def setup_inputs(seed: int = 0) -> dict:
    key = jax.random.key(seed)
    ks = jax.random.split(key, 31)
    L, Lv, D = DEPTH, DEPTH - 1, D_MODEL

    def nrm(k, shape, scale):
        return jax.random.normal(k, shape, jnp.float32) * scale

    def unif(k, shape, lo, hi):
        return jax.random.uniform(k, shape, jnp.float32, lo, hi)

    return {
        'x': nrm(ks[0], (BATCH, SEQ, D), 1.0),
        'c': nrm(ks[1], (BATCH, D), 1.0),
        'ada_w': nrm(ks[2], (L, D, N_MOD * D), 0.5 * D ** -0.5),
        'ada_b': nrm(ks[3], (L, N_MOD * D), 0.05),
        'norm_mix_g': 1.0 + nrm(ks[4], (L, D), 0.02),
        'w_in': nrm(ks[5], (L, D, IN_COLS), D ** -0.5),
        'w_in_vres': nrm(ks[6], (Lv, D, MV_LORA), D ** -0.5),
        'gla_gk_up': nrm(ks[7], (L, GLA_GATE_RANK, GLA_KW), GLA_GATE_RANK ** -0.5),
        'gla_gk_b': nrm(ks[8], (L, GLA_KW), 0.5),
        'gla_norm_g': 1.0 + nrm(ks[9], (L, GLA_DV), 0.02),
        'rwkv_mu': unif(ks[10], (L, RWKV_COLS), 0.0, 1.0),
        'rwkv_mu_vres': unif(ks[11], (Lv, MV_LORA), 0.0, 1.0),
        'w_lora_up': nrm(ks[12], (L, DECAY_LORA, RWKV_VW), 0.1 * DECAY_LORA ** -0.5),
        'w0': unif(ks[13], (L, RWKV_VW), -6.5, -1.5),
        'a_lora_up': nrm(ks[14], (L, AAA_LORA, RWKV_VW), AAA_LORA ** -0.5),
        'a0': nrm(ks[15], (L, RWKV_VW), 0.1),
        'g_lora_up': nrm(ks[16], (L, GATE_LORA, RWKV_VW), GATE_LORA ** -0.5),
        'v_lora_up': nrm(ks[17], (Lv, MV_LORA, RWKV_VW), MV_LORA ** -0.5),
        'v0': 1.0 + nrm(ks[18], (Lv, RWKV_VW), 0.1),
        'k_k': 0.85 + nrm(ks[19], (L, RWKV_VW), 0.02),
        'k_a': 1.0 + nrm(ks[20], (L, RWKV_VW), 0.02),
        'r_k': nrm(ks[21], (L, RWKV_HEADS, RWKV_HEAD), 0.1),
        'gn_g': 1.0 + nrm(ks[22], (L, RWKV_VW), 0.02),
        'gn_b': nrm(ks[23], (L, RWKV_VW), 0.02),
        'w_out': nrm(ks[24], (L, D_MIX, D), D_MIX ** -0.5),
        'norm_ffn_g': 1.0 + nrm(ks[25], (L, D), 0.02),
        'ffn_up': nrm(ks[26], (L, D, 2 * D_FF), D ** -0.5),
        'ffn_conv_w': nrm(ks[27], (L, CONV_W, 2 * D_FF), CONV_W ** -0.5),
        'ffn_conv_b': nrm(ks[28], (L, 2 * D_FF), 0.02),
        'ffn_down': nrm(ks[29], (L, D_FF, D), D_FF ** -0.5),
        'final_g': 1.0 + nrm(ks[30], (D,), 0.02),
    }


def reference(x, c, ada_w, ada_b, norm_mix_g, w_in, w_in_vres, gla_gk_up, gla_gk_b, gla_norm_g,
              rwkv_mu, rwkv_mu_vres, w_lora_up, w0, a_lora_up, a0, g_lora_up, v_lora_up, v0,
              k_k, k_a, r_k, gn_g, gn_b, w_out, norm_ffn_g, ffn_up, ffn_conv_w, ffn_conv_b,
              ffn_down, final_g):
    v_first = None
    for l in range(DEPTH):
        mod = jax.nn.silu(c) @ ada_w[l] + ada_b[l]
        sh_m, sc_m, gt_m, sh_f, sc_f, gt_f = jnp.split(mod, N_MOD, axis=-1)
        h = modulate(x, norm_mix_g[l], sh_m, sc_m)
        if l == 0:
            w_proj, mu, v_up_l, v0_l = w_in[0], rwkv_mu[0], None, None
        else:
            w_proj = jnp.concatenate([w_in[l], w_in_vres[l - 1]], axis=1)
            mu = jnp.concatenate([rwkv_mu[l], rwkv_mu_vres[l - 1]], axis=0)
            v_up_l, v0_l = v_lora_up[l - 1], v0[l - 1]
        proj = h @ w_proj
        o_gla = gla_group(proj[..., :GLA_COLS], gla_gk_up[l], gla_gk_b[l], gla_norm_g[l])
        o_rwkv, v_first = rwkv7_group(proj[..., GLA_COLS:], mu, w_lora_up[l], w0[l], a_lora_up[l], a0[l],
                                      g_lora_up[l], k_k[l], k_a[l], r_k[l], gn_g[l], gn_b[l],
                                      v_first, v_up_l, v0_l)
        mixed = jnp.concatenate([o_gla, o_rwkv], axis=-1) @ w_out[l]
        x = x + gt_m[:, None, :] * mixed
        h = modulate(x, norm_ffn_g[l], sh_f, sc_f)
        x = x + gt_f[:, None, :] * conv_ffn(h, ffn_up[l], ffn_conv_w[l], ffn_conv_b[l], ffn_down[l])
    return rms_norm(x, final_g)
```

```python
import functools

import jax
import jax.numpy as jnp
from jax import lax
from jax.experimental import pallas as pl
from jax.experimental.pallas import tpu as pltpu

F32 = jnp.float32
BF16 = jnp.bfloat16

GLA_HEADS = 4
GLA_DK = 64
GLA_DV = 128
GLA_KW = GLA_HEADS * GLA_DK
GLA_VW = GLA_HEADS * GLA_DV
GLA_GATE_RANK = 16
GLA_GATE_NORM = 16.0
RWKV_HEAD = 64
RWKV_VW = 512
DECAY_LORA = 64
AAA_LORA = 64
GATE_LORA = 128
MV_LORA = 32
GN_EPS = 64e-5
NORM_EPS = 1e-6
N_MOD = 6
CONV_W = 3

LANE = 128
QUAD = 256
CHUNK = 64
HALF = CHUNK // 2
GLA_COLS_PAD = 2 * GLA_KW + 2 * GLA_VW + LANE
RWKV_COLS_PAD = 3 * RWKV_VW + DECAY_LORA + AAA_LORA + GATE_LORA + LANE
VMEM_LIMIT = 56 * 1024 * 1024


def _dot(a, b):
    return jnp.dot(a, b, preferred_element_type=F32)


def _dot_nt(a, b):
    return lax.dot_general(a, b, (((1,), (1,)), ((), ())), preferred_element_type=F32)


def _dot_tn(a, b):
    return lax.dot_general(a, b, (((0,), (0,)), ((), ())), preferred_element_type=F32)


def _split(x, terms):
    parts = []
    rem = x
    for _ in range(terms):
        p = rem.astype(BF16)
        parts.append(p)
        rem = rem - p.astype(F32)
    return parts


def _dot_exact_rhs(x, w_bf16, terms):
    acc = None
    for p in _split(x, terms):
        d = _dot(p, w_bf16)
        acc = d if acc is None else acc + d
    return acc


def _cumsum_rows(tril_bf16, x, terms=3):
    acc = None
    for p in _split(x, terms):
        d = _dot(tril_bf16, p)
        acc = d if acc is None else acc + d
    return acc


def _silu(x):
    return x * jax.nn.sigmoid(x)


def _softplus(x):
    return jnp.maximum(x, 0.0) + jnp.log1p(jnp.exp(-jnp.abs(x)))


def _tile4(x):
    return jnp.concatenate([x, x, x, x], axis=0)


def _mod_kernel(c_ref, w_ref, b_ref, o_ref):
    s = _silu(c_ref[...])
    o_ref[...] = jnp.dot(s, w_ref[...], precision=lax.Precision.HIGHEST,
                         preferred_element_type=F32) + b_ref[...]


def _adaln(c, ada_w, ada_b):
    L, D, N = ada_w.shape
    B = c.shape[0]
    rows = 8 * ((B + 7) // 8)
    cp = jnp.zeros((rows, D), F32).at[:B].set(c)
    tn = 1536
    assert N % tn == 0
    out = pl.pallas_call(
        _mod_kernel,
        grid=(L, N // tn),
        in_specs=[
            pl.BlockSpec((rows, D), lambda l, n: (0, 0)),
            pl.BlockSpec((None, D, tn), lambda l, n: (l, 0, n)),
            pl.BlockSpec((None, 1, tn), lambda l, n: (l, 0, n)),
        ],
        out_specs=pl.BlockSpec((None, rows, tn), lambda l, n: (l, 0, n)),
        out_shape=jax.ShapeDtypeStruct((L, rows, N), F32),
        compiler_params=pltpu.CompilerParams(
            dimension_semantics=("parallel", "parallel"), vmem_limit_bytes=VMEM_LIMIT),
        name="adaln_mod",
    )(cp, ada_w, ada_b.reshape(L, 1, N))
    return out[:, :B]


def _modulated_norm(x, g, shift, scale):
    ms = jnp.mean(x * x, axis=-1, keepdims=True)
    y = x * lax.rsqrt(ms + NORM_EPS) * g
    return y * (1.0 + scale) + shift


def _mix_in_kernel(x_ref, sh_ref, sc_ref, g_ref, wg_ref, wr_ref, og_ref, or_ref):
    h = _modulated_norm(x_ref[...], g_ref[...], sh_ref[...], sc_ref[...]).astype(BF16)
    og_ref[...] = _dot(h, wg_ref[...])
    or_ref[...] = _dot(h, wr_ref[...])


def _mix_in(x, mod, g, w_gla, w_rwkv, tm):
    B, T, D = x.shape
    const = lambda b, t: (0, 0)
    return pl.pallas_call(
        _mix_in_kernel,
        grid=(B, T // tm),
        in_specs=[
            pl.BlockSpec((None, tm, D), lambda b, t: (b, t, 0)),
            pl.BlockSpec((None, None, 1, D), lambda b, t: (0, b, 0, 0)),
            pl.BlockSpec((None, None, 1, D), lambda b, t: (1, b, 0, 0)),
            pl.BlockSpec((1, D), const),
            pl.BlockSpec((D, GLA_COLS_PAD), const, pipeline_mode=pl.Buffered(1)),
            pl.BlockSpec((D, RWKV_COLS_PAD), const, pipeline_mode=pl.Buffered(1)),
        ],
        out_specs=[
            pl.BlockSpec((None, tm, GLA_COLS_PAD), lambda b, t: (b, t, 0)),
            pl.BlockSpec((None, tm, RWKV_COLS_PAD), lambda b, t: (b, t, 0)),
        ],
        out_shape=[
            jax.ShapeDtypeStruct((B, T, GLA_COLS_PAD), F32),
            jax.ShapeDtypeStruct((B, T, RWKV_COLS_PAD), F32),
        ],
        compiler_params=pltpu.CompilerParams(
            dimension_semantics=("parallel", "parallel"), vmem_limit_bytes=VMEM_LIMIT),
        name="mix_in",
    )(x, mod, mod, g, w_gla, w_rwkv)


def _gla_kernel(p_ref, gkup_ref, gkb_ref, ng_ref, tril_ref, bdk_ref, bdv_ref, bdvt_ref, causal_ref,
                o_ref, st_ref, obuf_ref, *, tc):
    @pl.when(pl.program_id(1) == 0)
    def _():
        st_ref[...] = jnp.zeros_like(st_ref)

    q_all = p_ref[:, 0:GLA_KW] * (GLA_DK ** -0.5)
    k_all = p_ref[:, GLA_KW:2 * GLA_KW]
    z = p_ref[:, 2 * GLA_KW + 2 * GLA_VW:GLA_COLS_PAD]
    pre = jnp.dot(z, gkup_ref[...], precision=lax.Precision.HIGHEST,
                  preferred_element_type=F32) + gkb_ref[...]
    gk = -_softplus(-pre) * (1.0 / GLA_GATE_NORM)
    g_all = _cumsum_rows(tril_ref[...], gk)

    bdk = bdk_ref[...]
    bdv = bdv_ref[...]
    bdvt = bdvt_ref[...]
    causal = causal_ref[...]
    first_half = lax.broadcasted_iota(jnp.int32, (CHUNK, 1), 0) < HALF

    for c in range(tc // CHUNK):
        r0 = c * CHUNK
        gc = g_all[r0:r0 + CHUNK]
        qc = q_all[r0:r0 + CHUNK]
        kc = k_all[r0:r0 + CHUNK]
        vc = p_ref[r0:r0 + CHUNK, 2 * GLA_KW:2 * GLA_KW + GLA_VW].astype(BF16)
        g_half = gc[HALF - 1:HALF]
        g_last = gc[CHUNK - 1:CHUNK]
        ref = jnp.where(first_half, 0.0, g_half)
        qp = (qc * jnp.exp(gc - ref)).astype(BF16)
        k0 = jnp.where(first_half, kc * jnp.exp(-gc), 0.0).astype(BF16)
        k1 = (kc * jnp.exp(g_half - gc)).astype(BF16)
        a0 = _dot_nt(qp[0:HALF], _tile4(k0) * bdk)
        a1 = _dot_nt(qp[HALF:CHUNK], _tile4(k1) * bdk)
        a = (jnp.concatenate([a0, a1], axis=0) * causal).astype(BF16)
        o = _dot(a, _tile4(vc) * bdv)
        qi = (qc * jnp.exp(gc)).astype(BF16)
        o = o + _dot_nt(qi, st_ref[...].astype(BF16))
        obuf_ref[r0:r0 + CHUNK, :] = o
        kd = (kc * jnp.exp(g_last - gc)).astype(BF16)
        upd = _dot_tn(vc, kd)
        st_ref[...] = st_ref[...] * jnp.exp(g_last) + upd * bdvt

    ng = ng_ref[...]
    for h in range(GLA_HEADS):
        sl = slice(h * GLA_DV, (h + 1) * GLA_DV)
        oh = obuf_ref[:, sl]
        ms = jnp.mean(oh * oh, axis=-1, keepdims=True)
        on = oh * lax.rsqrt(ms + NORM_EPS) * ng
        gate = p_ref[:, 2 * GLA_KW + GLA_VW + h * GLA_DV:2 * GLA_KW + GLA_VW + (h + 1) * GLA_DV]
        o_ref[:, sl] = on * _silu(gate)


def _gla(pg, gk_up_pad, gk_b, norm_g, consts, tc):
    B, T, _ = pg.shape
    const = lambda b, t: (0, 0)
    return pl.pallas_call(
        functools.partial(_gla_kernel, tc=tc),
        grid=(B, T // tc),
        in_specs=[
            pl.BlockSpec((None, tc, GLA_COLS_PAD), lambda b, t: (b, t, 0)),
            pl.BlockSpec((LANE, GLA_KW), const),
            pl.BlockSpec((1, GLA_KW), const),
            pl.BlockSpec((1, GLA_DV), const),
            pl.BlockSpec((tc, tc), const),
            pl.BlockSpec((QUAD, QUAD), const),
            pl.BlockSpec((QUAD, GLA_VW), const),
            pl.BlockSpec((GLA_VW, QUAD), const),
            pl.BlockSpec((CHUNK, QUAD), const),
        ],
        out_specs=pl.BlockSpec((None, tc, GLA_VW), lambda b, t: (b, t, 0)),
        out_shape=jax.ShapeDtypeStruct((B, T, GLA_VW), F32),
        scratch_shapes=[pltpu.VMEM((GLA_VW, GLA_KW), F32), pltpu.VMEM((tc, GLA_VW), F32)],
        compiler_params=pltpu.CompilerParams(
            dimension_semantics=("parallel", "arbitrary"), vmem_limit_bytes=VMEM_LIMIT),
        name="gla",
    )(pg, gk_up_pad, gk_b, norm_g, consts["tril"], consts["bd"], consts["bdv"], consts["bdvt"],
      consts["causal"])


_W0, _A0, _V0, _KK, _KA, _RK, _GNG, _GNB = range(8)


def _rwkv_kernel(*refs, tc, has_vfirst):
    if has_vfirst:
        (p_ref, mu_ref, vp_ref, wup_ref, aup_ref, gup_ref, vup_ref, vf_ref, tril_ref, ones_ref,
         bd_ref, strict_ref, incl_ref, o_ref, carry_ref, s_ref, ybuf_ref) = refs
    else:
        (p_ref, mu_ref, vp_ref, wup_ref, aup_ref, gup_ref, tril_ref, ones_ref,
         bd_ref, strict_ref, incl_ref, o_ref, vout_ref, carry_ref, s_ref, ybuf_ref) = refs

    @pl.when(pl.program_id(1) == 0)
    def _():
        carry_ref[...] = jnp.zeros_like(carry_ref)
        s_ref[...] = jnp.zeros_like(s_ref)

    p = p_ref[...]
    row = lax.broadcasted_iota(jnp.int32, (tc, 1), 0)
    prev = jnp.where(row == 0, carry_ref[7:8, :], pltpu.roll(p, 1, 0))
    carry_ref[...] = p[tc - 8:tc]
    pm = p + (prev - p) * mu_ref[...]

    V = RWKV_VW
    r = pm[:, 0:V]
    k = pm[:, V:2 * V]
    v = pm[:, 2 * V:3 * V]
    lora_wa = pm[:, 3 * V:3 * V + LANE]
    lora_g = pm[:, 3 * V + LANE:3 * V + 2 * LANE]
    vp = vp_ref[...]

    w_log = -_softplus(-(vp[_W0:_W0 + 1] + _dot(jnp.tanh(lora_wa).astype(BF16), wup_ref[...]))) - 0.5
    lw = -jnp.exp(w_log)
    a_lr = jax.nn.sigmoid(vp[_A0:_A0 + 1] + _dot(lora_wa.astype(BF16), aup_ref[...]))
    g = _dot(jax.nn.sigmoid(lora_g).astype(BF16), gup_ref[...])
    if has_vfirst:
        lora_v = pm[:, 3 * V + 2 * LANE:3 * V + 3 * LANE]
        mix = jax.nn.sigmoid(vp[_V0:_V0 + 1] + _dot(lora_v.astype(BF16), vup_ref[...]))
        v = v + (vf_ref[...] - v) * mix
    else:
        vout_ref[...] = v

    ones_bd = ones_ref[...]
    kk = k * vp[_KK:_KK + 1]
    ss = _dot_exact_rhs(kk * kk, ones_bd, 2)
    kk = kk * lax.rsqrt(jnp.maximum(ss, 1e-24))
    k = k * (1.0 + (a_lr - 1.0) * vp[_KA:_KA + 1])

    cum = _cumsum_rows(tril_ref[...], lw)
    w_inv = jnp.exp(-cum)
    rt = (r * jnp.exp(cum)).astype(BF16)
    kt = (k * w_inv).astype(BF16)
    bt = (kk * a_lr * w_inv).astype(BF16)
    at = (-kk * jnp.exp(cum - lw)).astype(BF16)
    vb = v.astype(BF16)

    bd = bd_ref[...]
    strict = strict_ref[...]
    incl = incl_ref[...]

    def blockdiag(x):
        return _tile4(x.astype(BF16)) * bd

    for c in range(tc // CHUNK):
        rs = slice(c * CHUNK, (c + 1) * CHUNK)
        wc = jnp.exp(cum[(c + 1) * CHUNK - 1:(c + 1) * CHUNK])
        for q in range(V // QUAD):
            ls = slice(q * QUAD, (q + 1) * QUAD)
            s_q = s_ref[q]
            x = jnp.concatenate([at[rs, ls], rt[rs, ls]], axis=0)
            pb = _dot_nt(x, blockdiag(bt[rs, ls]))
            pk = _dot_nt(x, blockdiag(kt[rs, ls]))
            a_ab = pb[0:CHUNK] * strict
            a_rb = pb[CHUNK:] * incl
            a_k = jnp.concatenate([pk[0:CHUNK] * strict, pk[CHUNK:] * incl], axis=0)
            xs = _dot_nt(x, s_q.astype(BF16))
            av = _dot(a_k.astype(BF16), blockdiag(vb[rs, ls]))
            u = xs[0:CHUNK] + av[0:CHUNK]
            pw = a_ab
            for i in range(6):
                u = u + _dot(pw.astype(BF16), blockdiag(u))
                if i < 5:
                    pw = _dot(pw.astype(BF16), blockdiag(pw))
            y = xs[CHUNK:] + av[CHUNK:] + _dot(a_rb.astype(BF16), blockdiag(u))
            ybuf_ref[rs, ls] = y
            uv = jnp.concatenate([u.astype(BF16), vb[rs, ls]], axis=0)
            bk = jnp.concatenate([bt[rs, ls], kt[rs, ls]], axis=0)
            s_ref[q] = (s_q + _dot_tn(uv, bk) * bd.astype(F32)) * wc[:, ls]

    y = ybuf_ref[...]
    inv_n = 1.0 / RWKV_HEAD
    mean = _dot_exact_rhs(y, ones_bd, 2) * inv_n
    yc = y - mean
    var = _dot_exact_rhs(yc * yc, ones_bd, 2) * inv_n
    yn = yc * lax.rsqrt(var + GN_EPS) * vp[_GNG:_GNG + 1] + vp[_GNB:_GNB + 1]
    bonus = _dot_exact_rhs(r * k * vp[_RK:_RK + 1], ones_bd, 2) * v
    o_ref[...] = (yn + bonus) * g


def _rwkv(pr, mu, vparams, w_up, a_up, g_up, v_up, v_first, consts, tc):
    B, T, _ = pr.shape
    const = lambda b, t: (0, 0)
    has_vfirst = v_first is not None
    tok = lambda w: pl.BlockSpec((None, tc, w), lambda b, t: (b, t, 0))
    in_specs = [
        tok(RWKV_COLS_PAD),
        pl.BlockSpec((1, RWKV_COLS_PAD), const),
        pl.BlockSpec((8, RWKV_VW), const),
        pl.BlockSpec((LANE, RWKV_VW), const),
        pl.BlockSpec((LANE, RWKV_VW), const),
        pl.BlockSpec((LANE, RWKV_VW), const),
    ]
    args = [pr, mu, vparams, w_up, a_up, g_up]
    if has_vfirst:
        in_specs += [pl.BlockSpec((LANE, RWKV_VW), const), tok(RWKV_VW)]
        args += [v_up, v_first]
    in_specs += [
        pl.BlockSpec((tc, tc), const),
        pl.BlockSpec((RWKV_VW, RWKV_VW), const),
        pl.BlockSpec((QUAD, QUAD), const),
        pl.BlockSpec((CHUNK, QUAD), const),
        pl.BlockSpec((CHUNK, QUAD), const),
    ]
    args += [consts["tril"], consts["ones_bd"], consts["bd"], consts["strict"], consts["causal"]]
    out_sds = jax.ShapeDtypeStruct((B, T, RWKV_VW), F32)
    if has_vfirst:
        out_specs, out_shape = tok(RWKV_VW), out_sds
    else:
        out_specs, out_shape = [tok(RWKV_VW), tok(RWKV_VW)], [out_sds, out_sds]
    res = pl.pallas_call(
        functools.partial(_rwkv_kernel, tc=tc, has_vfirst=has_vfirst),
        grid=(B, T // tc),
        in_specs=in_specs,
        out_specs=out_specs,
        out_shape=out_shape,
        scratch_shapes=[
            pltpu.VMEM((8, RWKV_COLS_PAD), F32),
            pltpu.VMEM((RWKV_VW // QUAD, QUAD, QUAD), F32),
            pltpu.VMEM((tc, RWKV_VW), F32),
        ],
        compiler_params=pltpu.CompilerParams(
            dimension_semantics=("parallel", "arbitrary"), vmem_limit_bytes=VMEM_LIMIT),
        name="rwkv7",
    )(*args)
    if has_vfirst:
        return res, v_first
    return res[0], res[1]


def _ffn_kernel(x_ref, og_ref, or_ref, wo_g_ref, wo_r_ref, gtm_ref, shf_ref, scf_ref, gtf_ref,
                nfg_ref, wup_ref, cw_ref, cb_ref, wdn_ref, fg_ref, out_ref, tail_ref,
                *, tm, fc, d_ff, final):
    @pl.when(pl.program_id(1) == 0)
    def _():
        tail_ref[...] = jnp.zeros_like(tail_ref)

    mixed = _dot(og_ref[...].astype(BF16), wo_g_ref[...]) + _dot(or_ref[...].astype(BF16), wo_r_ref[...])
    x1 = x_ref[...] + gtm_ref[...] * mixed
    h = _modulated_norm(x1, nfg_ref[...], shf_ref[...], scf_ref[...]).astype(BF16)

    row = lax.broadcasted_iota(jnp.int32, (tm, 1), 0)

    def conv_part(cols):
        u = _dot(h, wup_ref[:, cols])
        tl = tail_ref[:, cols]
        u1 = jnp.where(row == 0, tl[7:8], pltpu.roll(u, 1, 0))
        u2 = jnp.where(row == 0, tl[6:7], jnp.where(row == 1, tl[7:8], pltpu.roll(u, 2, 0)))
        tail_ref[:, cols] = u[tm - 8:tm]
        return cw_ref[0:1, cols] * u2 + cw_ref[1:2, cols] * u1 + cw_ref[2:3, cols] * u + cb_ref[:, cols]

    acc = jnp.zeros((tm, x_ref.shape[-1]), F32)
    for j in range(d_ff // fc):
        gate = conv_part(slice(j * fc, (j + 1) * fc))
        val = conv_part(slice(d_ff + j * fc, d_ff + (j + 1) * fc))
        act = (_silu(gate) * val).astype(BF16)
        acc = acc + _dot(act, wdn_ref[j * fc:(j + 1) * fc, :])
    x2 = x1 + gtf_ref[...] * acc
    if final:
        ms = jnp.mean(x2 * x2, axis=-1, keepdims=True)
        x2 = x2 * lax.rsqrt(ms + NORM_EPS) * fg_ref[...]
    out_ref[...] = x2


def _ffn(x, o_gla, o_rwkv, wo_g, wo_r, mod, nfg, w_up, conv_w, conv_b, w_down, final_g, final, tm, fc):
    B, T, D = x.shape
    d_ff = w_down.shape[0]
    const = lambda b, t: (0, 0)
    tok = lambda w: pl.BlockSpec((None, tm, w), lambda b, t: (b, t, 0))
    modspec = lambda i: pl.BlockSpec((None, None, 1, D), lambda b, t: (i, b, 0, 0))
    single = pl.Buffered(1)
    return pl.pallas_call(
        functools.partial(_ffn_kernel, tm=tm, fc=fc, d_ff=d_ff, final=final),
        grid=(B, T // tm),
        in_specs=[
            tok(D), tok(GLA_VW), tok(RWKV_VW),
            pl.BlockSpec((GLA_VW, D), const, pipeline_mode=single),
            pl.BlockSpec((RWKV_VW, D), const, pipeline_mode=single),
            modspec(2), modspec(3), modspec(4), modspec(5),
            pl.BlockSpec((1, D), const),
            pl.BlockSpec((D, 2 * d_ff), const, pipeline_mode=single),
            pl.BlockSpec((CONV_W, 2 * d_ff), const),
            pl.BlockSpec((1, 2 * d_ff), const),
            pl.BlockSpec((d_ff, D), const, pipeline_mode=single),
            pl.BlockSpec((1, D), const),
        ],
        out_specs=tok(D),
        out_shape=jax.ShapeDtypeStruct((B, T, D), F32),
        scratch_shapes=[pltpu.VMEM((8, 2 * d_ff), F32)],
        compiler_params=pltpu.CompilerParams(
            dimension_semantics=("parallel", "arbitrary"), vmem_limit_bytes=VMEM_LIMIT),
        name="out_ffn",
    )(x, o_gla, o_rwkv, wo_g, wo_r, mod, mod, mod, mod, nfg, w_up, conv_w, conv_b, w_down, final_g)


def _masks(tc):
    i = jnp.arange(tc)
    tril = ((i[:, None] >= i[None, :]) & (i[:, None] // CHUNK == i[None, :] // CHUNK)).astype(BF16)
    hq = jnp.arange(QUAD) // RWKV_HEAD
    bd = (hq[:, None] == hq[None, :]).astype(BF16)
    hv = jnp.arange(GLA_VW) // GLA_DV
    bdv = (hq[:, None] == hv[None, :]).astype(BF16)
    ho = jnp.arange(RWKV_VW) // RWKV_HEAD
    ones_bd = (ho[:, None] == ho[None, :]).astype(BF16)
    ci = jnp.arange(CHUNK)[:, None]
    cj = (jnp.arange(QUAD) % CHUNK)[None, :]
    return dict(tril=tril, bd=bd, bdv=bdv, bdvt=bdv.T.astype(F32), ones_bd=ones_bd,
                causal=(ci >= cj).astype(F32), strict=(ci > cj).astype(F32))


def _pad_rows(w, rows, at):
    out = jnp.zeros((rows, w.shape[1]), w.dtype)
    return out.at[at:at + w.shape[0]].set(w)


def kernel(x, c, ada_w, ada_b, norm_mix_g, w_in, w_in_vres, gla_gk_up, gla_gk_b, gla_norm_g,
           rwkv_mu, rwkv_mu_vres, w_lora_up, w0, a_lora_up, a0, g_lora_up, v_lora_up, v0,
           k_k, k_a, r_k, gn_g, gn_b, w_out, norm_ffn_g, ffn_up, ffn_conv_w, ffn_conv_b,
           ffn_down, final_g):
    B, T, D = x.shape
    L = ada_w.shape[0]
    gla_cols = 2 * GLA_KW + 2 * GLA_VW + GLA_GATE_RANK
    tm_in = min(256, T)
    tc = min(256, T)
    tm_ffn = min(512, T)
    fc = 256
    consts = _masks(tc)

    mod_all = _adaln(c, ada_w, ada_b)
    v_first = None
    for l in range(L):
        mod = mod_all[l].reshape(B, N_MOD, 1, D).transpose(1, 0, 2, 3)
        w_gla = jnp.pad(w_in[l][:, :gla_cols], ((0, 0), (0, GLA_COLS_PAD - gla_cols))).astype(BF16)
        w_r = w_in[l][:, gla_cols:]
        mu = rwkv_mu[l]
        if l > 0:
            w_r = jnp.concatenate([w_r, w_in_vres[l - 1]], axis=1)
            mu = jnp.concatenate([mu, rwkv_mu_vres[l - 1]], axis=0)
        w_rwkv = jnp.pad(w_r, ((0, 0), (0, RWKV_COLS_PAD - w_r.shape[1]))).astype(BF16)
        mu = jnp.pad(mu, (0, RWKV_COLS_PAD - mu.shape[0])).reshape(1, RWKV_COLS_PAD)

        pg, pr = _mix_in(x, mod, norm_mix_g[l].reshape(1, D), w_gla, w_rwkv, tm_in)

        o_gla = _gla(pg, _pad_rows(gla_gk_up[l], LANE, 0), gla_gk_b[l].reshape(1, GLA_KW),
                     gla_norm_g[l].reshape(1, GLA_DV), consts, tc)

        v0_l = v0[l - 1] if l > 0 else jnp.zeros((RWKV_VW,), F32)
        vparams = jnp.stack([w0[l], a0[l], v0_l, k_k[l], k_a[l], r_k[l].reshape(-1), gn_g[l], gn_b[l]])
        v_up = _pad_rows(v_lora_up[l - 1], LANE, 0).astype(BF16) if l > 0 else None
        o_rwkv, v_first = _rwkv(
            pr, mu, vparams,
            _pad_rows(w_lora_up[l], LANE, 0).astype(BF16),
            _pad_rows(a_lora_up[l], LANE, DECAY_LORA).astype(BF16),
            g_lora_up[l].astype(BF16), v_up, v_first, consts, tc)

        x = _ffn(x, o_gla, o_rwkv, w_out[l][:GLA_VW].astype(BF16), w_out[l][GLA_VW:].astype(BF16),
                 mod, norm_ffn_g[l].reshape(1, D), ffn_up[l].astype(BF16), ffn_conv_w[l],
                 ffn_conv_b[l].reshape(1, -1), ffn_down[l].astype(BF16), final_g.reshape(1, D),
                 l == L - 1, tm_ffn, fc)
    return x
```

```python
import functools

import jax
import jax.numpy as jnp
from jax import lax
from jax.experimental import pallas as pl
from jax.experimental.pallas import tpu as pltpu

F32 = jnp.float32
BF16 = jnp.bfloat16

GLA_HEADS = 4
GLA_DK = 64
GLA_DV = 128
GLA_KW = GLA_HEADS * GLA_DK
GLA_VW = GLA_HEADS * GLA_DV
GLA_GATE_RANK = 16
GLA_GATE_NORM = 16.0
RWKV_HEAD = 64
RWKV_VW = 512
DECAY_LORA = 64
AAA_LORA = 64
GATE_LORA = 128
MV_LORA = 32
GN_EPS = 64e-5
NORM_EPS = 1e-6
N_MOD = 6
CONV_W = 3

LANE = 128
QUAD = 256
CHUNK = 64
HALF = CHUNK // 2
GLA_COLS_PAD = 2 * GLA_KW + 2 * GLA_VW + LANE
RWKV_COLS_PAD = 3 * RWKV_VW + DECAY_LORA + AAA_LORA + GATE_LORA + LANE
VMEM_LIMIT = 56 * 1024 * 1024


def _dot(a, b):
    return jnp.dot(a, b, preferred_element_type=F32)


def _dot_nt(a, b):
    return lax.dot_general(a, b, (((1,), (1,)), ((), ())), preferred_element_type=F32)


def _dot_tn(a, b):
    return lax.dot_general(a, b, (((0,), (0,)), ((), ())), preferred_element_type=F32)


def _split(x, terms):
    parts = []
    rem = x
    for _ in range(terms):
        p = rem.astype(BF16)
        parts.append(p)
        rem = rem - p.astype(F32)
    return parts


def _dot_exact_rhs(x, w_bf16, terms):
    acc = None
    for p in _split(x, terms):
        d = _dot(p, w_bf16)
        acc = d if acc is None else acc + d
    return acc


def _cumsum_rows(tril_bf16, x, terms=3):
    acc = None
    for p in _split(x, terms):
        d = _dot(tril_bf16, p)
        acc = d if acc is None else acc + d
    return acc


def _silu(x):
    return x * jax.nn.sigmoid(x)


def _softplus(x):
    return jnp.maximum(x, 0.0) + jnp.log1p(jnp.exp(-jnp.abs(x)))


def _tile4(x):
    return jnp.concatenate([x, x, x, x], axis=0)


def _mod_kernel(c_ref, w_ref, b_ref, o_ref):
    s = _silu(c_ref[...])
    o_ref[...] = jnp.dot(s, w_ref[...], precision=lax.Precision.HIGHEST,
                         preferred_element_type=F32) + b_ref[...]


def _adaln(c, ada_w, ada_b):
    L, D, N = ada_w.shape
    B = c.shape[0]
    rows = 8 * ((B + 7) // 8)
    cp = jnp.zeros((rows, D), F32).at[:B].set(c)
    tn = 1536
    assert N % tn == 0
    out = pl.pallas_call(
        _mod_kernel,
        grid=(L, N // tn),
        in_specs=[
            pl.BlockSpec((rows, D), lambda l, n: (0, 0)),
            pl.BlockSpec((None, D, tn), lambda l, n: (l, 0, n)),
            pl.BlockSpec((None, 1, tn), lambda l, n: (l, 0, n)),
        ],
        out_specs=pl.BlockSpec((None, rows, tn), lambda l, n: (l, 0, n)),
        out_shape=jax.ShapeDtypeStruct((L, rows, N), F32),
        compiler_params=pltpu.CompilerParams(
            dimension_semantics=("parallel", "parallel"), vmem_limit_bytes=VMEM_LIMIT),
        name="adaln_mod",
    )(cp, ada_w, ada_b.reshape(L, 1, N))
    return out[:, :B]


def _modulated_norm(x, g, shift, scale):
    ms = jnp.mean(x * x, axis=-1, keepdims=True)
    y = x * lax.rsqrt(ms + NORM_EPS) * g
    return y * (1.0 + scale) + shift


def _mix_in_kernel(x_ref, sh_ref, sc_ref, g_ref, wg_ref, wr_ref, og_ref, or_ref):
    h = _modulated_norm(x_ref[...], g_ref[...], sh_ref[...], sc_ref[...]).astype(BF16)
    og_ref[...] = _dot(h, wg_ref[...])
    or_ref[...] = _dot(h, wr_ref[...])


def _mix_in(x, mod, g, w_gla, w_rwkv, tm):
    B, T, D = x.shape
    const = lambda b, t: (0, 0)
    return pl.pallas_call(
        _mix_in_kernel,
        grid=(B, T // tm),
        in_specs=[
            pl.BlockSpec((None, tm, D), lambda b, t: (b, t, 0)),
            pl.BlockSpec((None, None, 1, D), lambda b, t: (0, b, 0, 0)),
            pl.BlockSpec((None, None, 1, D), lambda b, t: (1, b, 0, 0)),
            pl.BlockSpec((1, D), const),
            pl.BlockSpec((D, GLA_COLS_PAD), const, pipeline_mode=pl.Buffered(1)),
            pl.BlockSpec((D, RWKV_COLS_PAD), const, pipeline_mode=pl.Buffered(1)),
        ],
        out_specs=[
            pl.BlockSpec((None, tm, GLA_COLS_PAD), lambda b, t: (b, t, 0)),
            pl.BlockSpec((None, tm, RWKV_COLS_PAD), lambda b, t: (b, t, 0)),
        ],
        out_shape=[
            jax.ShapeDtypeStruct((B, T, GLA_COLS_PAD), F32),
            jax.ShapeDtypeStruct((B, T, RWKV_COLS_PAD), F32),
        ],
        compiler_params=pltpu.CompilerParams(
            dimension_semantics=("parallel", "parallel"), vmem_limit_bytes=VMEM_LIMIT),
        name="mix_in",
    )(x, mod, mod, g, w_gla, w_rwkv)


def _gla_kernel(p_ref, gkup_ref, gkb_ref, ng_ref, tril_ref, bdk_ref, bdv_ref, bdvt_ref, causal_ref,
                o_ref, st_ref, obuf_ref, *, tc):
    @pl.when(pl.program_id(1) == 0)
    def _():
        st_ref[...] = jnp.zeros_like(st_ref)

    q_all = p_ref[:, 0:GLA_KW] * (GLA_DK ** -0.5)
    k_all = p_ref[:, GLA_KW:2 * GLA_KW]
    z = p_ref[:, 2 * GLA_KW + 2 * GLA_VW:GLA_COLS_PAD]
    pre = jnp.dot(z, gkup_ref[...], precision=lax.Precision.HIGHEST,
                  preferred_element_type=F32) + gkb_ref[...]
    gk = -_softplus(-pre) * (1.0 / GLA_GATE_NORM)
    g_all = _cumsum_rows(tril_ref[...], gk)

    bdk = bdk_ref[...]
    bdv = bdv_ref[...]
    bdvt = bdvt_ref[...]
    causal = causal_ref[...]
    first_half = lax.broadcasted_iota(jnp.int32, (CHUNK, 1), 0) < HALF

    for c in range(tc // CHUNK):
        r0 = c * CHUNK
        gc = g_all[r0:r0 + CHUNK]
        qc = q_all[r0:r0 + CHUNK]
        kc = k_all[r0:r0 + CHUNK]
        vc = p_ref[r0:r0 + CHUNK, 2 * GLA_KW:2 * GLA_KW + GLA_VW].astype(BF16)
        g_half = gc[HALF - 1:HALF]
        g_last = gc[CHUNK - 1:CHUNK]
        ref = jnp.where(first_half, 0.0, g_half)
        qp = (qc * jnp.exp(gc - ref)).astype(BF16)
        k0 = jnp.where(first_half, kc * jnp.exp(-gc), 0.0).astype(BF16)
        k1 = (kc * jnp.exp(g_half - gc)).astype(BF16)
        a0 = _dot_nt(qp[0:HALF], _tile4(k0) * bdk)
        a1 = _dot_nt(qp[HALF:CHUNK], _tile4(k1) * bdk)
        a = (jnp.concatenate([a0, a1], axis=0) * causal).astype(BF16)
        o = _dot(a, _tile4(vc) * bdv)
        qi = (qc * jnp.exp(gc)).astype(BF16)
        o = o + _dot_nt(qi, st_ref[...].astype(BF16))
        obuf_ref[r0:r0 + CHUNK, :] = o
        kd = (kc * jnp.exp(g_last - gc)).astype(BF16)
        upd = _dot_tn(vc, kd)
        st_ref[...] = st_ref[...] * jnp.exp(g_last) + upd * bdvt

    ng = ng_ref[...]
    for h in range(GLA_HEADS):
        sl = slice(h * GLA_DV, (h + 1) * GLA_DV)
        oh = obuf_ref[:, sl]
        ms = jnp.mean(oh * oh, axis=-1, keepdims=True)
        on = oh * lax.rsqrt(ms + NORM_EPS) * ng
        gate = p_ref[:, 2 * GLA_KW + GLA_VW + h * GLA_DV:2 * GLA_KW + GLA_VW + (h + 1) * GLA_DV]
        o_ref[:, sl] = on * _silu(gate)


def _gla(pg, gk_up_pad, gk_b, norm_g, consts, tc):
    B, T, _ = pg.shape
    const = lambda b, t: (0, 0)
    return pl.pallas_call(
        functools.partial(_gla_kernel, tc=tc),
        grid=(B, T // tc),
        in_specs=[
            pl.BlockSpec((None, tc, GLA_COLS_PAD), lambda b, t: (b, t, 0)),
            pl.BlockSpec((LANE, GLA_KW), const),
            pl.BlockSpec((1, GLA_KW), const),
            pl.BlockSpec((1, GLA_DV), const),
            pl.BlockSpec((tc, tc), const),
            pl.BlockSpec((QUAD, QUAD), const),
            pl.BlockSpec((QUAD, GLA_VW), const),
            pl.BlockSpec((GLA_VW, QUAD), const),
            pl.BlockSpec((CHUNK, QUAD), const),
        ],
        out_specs=pl.BlockSpec((None, tc, GLA_VW), lambda b, t: (b, t, 0)),
        out_shape=jax.ShapeDtypeStruct((B, T, GLA_VW), F32),
        scratch_shapes=[pltpu.VMEM((GLA_VW, GLA_KW), F32), pltpu.VMEM((tc, GLA_VW), F32)],
        compiler_params=pltpu.CompilerParams(
            dimension_semantics=("parallel", "arbitrary"), vmem_limit_bytes=VMEM_LIMIT),
        name="gla",
    )(pg, gk_up_pad, gk_b, norm_g, consts["tril"], consts["bd"], consts["bdv"], consts["bdvt"],
      consts["causal"])


_W0, _A0, _V0, _KK, _KA, _RK, _GNG, _GNB = range(8)


def _rwkv_kernel(*refs, tc, has_vfirst):
    if has_vfirst:
        (p_ref, mu_ref, vp_ref, wup_ref, aup_ref, gup_ref, vup_ref, vf_ref, tril_ref, ones_ref,
         bd_ref, bdf_ref, eye_ref, strict_ref, incl_ref, o_ref, carry_ref, s_ref, ybuf_ref) = refs
    else:
        (p_ref, mu_ref, vp_ref, wup_ref, aup_ref, gup_ref, tril_ref, ones_ref,
         bd_ref, bdf_ref, eye_ref, strict_ref, incl_ref, o_ref, vout_ref, carry_ref, s_ref,
         ybuf_ref) = refs

    @pl.when(pl.program_id(1) == 0)
    def _():
        carry_ref[...] = jnp.zeros_like(carry_ref)
        s_ref[...] = jnp.zeros_like(s_ref)

    p = p_ref[...]
    row = lax.broadcasted_iota(jnp.int32, (tc, 1), 0)
    prev = jnp.where(row == 0, carry_ref[7:8, :], pltpu.roll(p, 1, 0))
    carry_ref[...] = p[tc - 8:tc]
    pm = p + (prev - p) * mu_ref[...]

    V = RWKV_VW
    r = pm[:, 0:V]
    k = pm[:, V:2 * V]
    v = pm[:, 2 * V:3 * V]
    lora_wa = pm[:, 3 * V:3 * V + LANE]
    lora_g = pm[:, 3 * V + LANE:3 * V + 2 * LANE]
    vp = vp_ref[...]

    w_log = -_softplus(-(vp[_W0:_W0 + 1] + _dot(jnp.tanh(lora_wa).astype(BF16), wup_ref[...]))) - 0.5
    lw = -jnp.exp(w_log)
    a_lr = jax.nn.sigmoid(vp[_A0:_A0 + 1] + _dot(lora_wa.astype(BF16), aup_ref[...]))
    g = _dot(jax.nn.sigmoid(lora_g).astype(BF16), gup_ref[...])
    if has_vfirst:
        lora_v = pm[:, 3 * V + 2 * LANE:3 * V + 3 * LANE]
        mix = jax.nn.sigmoid(vp[_V0:_V0 + 1] + _dot(lora_v.astype(BF16), vup_ref[...]))
        v = v + (vf_ref[...] - v) * mix
    else:
        vout_ref[...] = v

    ones_bd = ones_ref[...]
    kk = k * vp[_KK:_KK + 1]
    ss = _dot_exact_rhs(kk * kk, ones_bd, 2)
    kk = kk * lax.rsqrt(jnp.maximum(ss, 1e-24))
    k = k * (1.0 + (a_lr - 1.0) * vp[_KA:_KA + 1])

    cum = _cumsum_rows(tril_ref[...], lw)
    w_inv = jnp.exp(-cum)
    rt = r * jnp.exp(cum)
    kt = (k * w_inv).astype(BF16)
    bt = (kk * a_lr * w_inv).astype(BF16)
    at = (-kk * jnp.exp(cum - lw)).astype(BF16)
    vb = v.astype(BF16)

    bd = bd_ref[...]
    bdf = bdf_ref[...]
    eye = eye_ref[...]
    strict = strict_ref[...]
    incl = incl_ref[...]

    def blockdiag(x):
        return _tile4(x.astype(BF16)) * bd

    n_chunks = tc // CHUNK
    n_quads = V // QUAD
    chains = [(c, q) for c in range(n_chunks) for q in range(n_quads)]
    rows = lambda c: slice(c * CHUNK, (c + 1) * CHUNK)
    lanes = lambda q: slice(q * QUAD, (q + 1) * QUAD)
    sl = {cq: (rows(cq[0]), lanes(cq[1])) for cq in chains}

    pb, pk = {}, {}
    for cq in chains:
        x = jnp.concatenate([at[sl[cq]], rt[sl[cq]].astype(BF16)], axis=0)
        pb[cq] = _dot_nt(x, blockdiag(bt[sl[cq]]))
        pk[cq] = _dot_nt(x, blockdiag(kt[sl[cq]]))
    t, pw, a_rb, av = {}, {}, {}, {}
    for cq in chains:
        a_ab = pb[cq][0:CHUNK] * strict
        a_rb[cq] = (pb[cq][CHUNK:] * incl).astype(BF16)
        a_k = jnp.concatenate([pk[cq][0:CHUNK] * strict, pk[cq][CHUNK:] * incl], axis=0)
        t[cq] = eye + a_ab
        pw[cq] = _dot(a_ab.astype(BF16), blockdiag(a_ab))
        av[cq] = _dot(a_k.astype(BF16), blockdiag(vb[sl[cq]]))
    for _ in range(4):
        for cq in chains:
            tp = _dot(jnp.concatenate([t[cq], pw[cq]], axis=0).astype(BF16), blockdiag(pw[cq]))
            t[cq] = t[cq] + tp[0:CHUNK]
            pw[cq] = tp[CHUNK:]
    for cq in chains:
        t[cq] = (t[cq] + _dot(t[cq].astype(BF16), blockdiag(pw[cq]))).astype(BF16)
    abar, u0 = {}, {}
    for cq in chains:
        abar[cq] = _dot(t[cq], blockdiag(at[sl[cq]]))
        u0[cq] = _dot(t[cq], blockdiag(av[cq][0:CHUNK]))
    rbar, y0, gm, nm = {}, {}, {}, {}
    for cq in chains:
        rbar[cq] = (rt[sl[cq]] + _dot(a_rb[cq], blockdiag(abar[cq]))).astype(BF16)
        y0[cq] = av[cq][CHUNK:] + _dot(a_rb[cq], blockdiag(u0[cq]))
    for cq in chains:
        gm[cq] = (_dot_tn(abar[cq].astype(BF16), bt[sl[cq]]) * bdf).astype(BF16)
        nm[cq] = _dot_tn(jnp.concatenate([u0[cq].astype(BF16), vb[sl[cq]]], axis=0),
                         jnp.concatenate([bt[sl[cq]], kt[sl[cq]]], axis=0)) * bdf
    s = [s_ref[q] for q in range(n_quads)]
    for c in range(n_chunks):
        wc = jnp.exp(cum[(c + 1) * CHUNK - 1:(c + 1) * CHUNK])
        for q in range(n_quads):
            cq = (c, q)
            sb = s[q].astype(BF16)
            ybuf_ref[sl[cq]] = _dot_nt(rbar[cq], sb) + y0[cq]
            s[q] = (s[q] + _dot(sb, gm[cq]) + nm[cq]) * wc[:, lanes(q)]
    for q in range(n_quads):
        s_ref[q] = s[q]

    y = ybuf_ref[...]
    inv_n = 1.0 / RWKV_HEAD
    mean = _dot_exact_rhs(y, ones_bd, 2) * inv_n
    yc = y - mean
    var = _dot_exact_rhs(yc * yc, ones_bd, 2) * inv_n
    yn = yc * lax.rsqrt(var + GN_EPS) * vp[_GNG:_GNG + 1] + vp[_GNB:_GNB + 1]
    bonus = _dot_exact_rhs(r * k * vp[_RK:_RK + 1], ones_bd, 2) * v
    o_ref[...] = (yn + bonus) * g


def _rwkv(pr, mu, vparams, w_up, a_up, g_up, v_up, v_first, consts, tc):
    B, T, _ = pr.shape
    const = lambda b, t: (0, 0)
    has_vfirst = v_first is not None
    tok = lambda w: pl.BlockSpec((None, tc, w), lambda b, t: (b, t, 0))
    in_specs = [
        tok(RWKV_COLS_PAD),
        pl.BlockSpec((1, RWKV_COLS_PAD), const),
        pl.BlockSpec((8, RWKV_VW), const),
        pl.BlockSpec((LANE, RWKV_VW), const),
        pl.BlockSpec((LANE, RWKV_VW), const),
        pl.BlockSpec((LANE, RWKV_VW), const),
    ]
    args = [pr, mu, vparams, w_up, a_up, g_up]
    if has_vfirst:
        in_specs += [pl.BlockSpec((LANE, RWKV_VW), const), tok(RWKV_VW)]
        args += [v_up, v_first]
    in_specs += [
        pl.BlockSpec((tc, tc), const),
        pl.BlockSpec((RWKV_VW, RWKV_VW), const),
        pl.BlockSpec((QUAD, QUAD), const),
        pl.BlockSpec((QUAD, QUAD), const),
        pl.BlockSpec((CHUNK, QUAD), const),
        pl.BlockSpec((CHUNK, QUAD), const),
        pl.BlockSpec((CHUNK, QUAD), const),
    ]
    args += [consts["tril"], consts["ones_bd"], consts["bd"], consts["bdf"], consts["eye"],
             consts["strict"], consts["causal"]]
    out_sds = jax.ShapeDtypeStruct((B, T, RWKV_VW), F32)
    if has_vfirst:
        out_specs, out_shape = tok(RWKV_VW), out_sds
    else:
        out_specs, out_shape = [tok(RWKV_VW), tok(RWKV_VW)], [out_sds, out_sds]
    res = pl.pallas_call(
        functools.partial(_rwkv_kernel, tc=tc, has_vfirst=has_vfirst),
        grid=(B, T // tc),
        in_specs=in_specs,
        out_specs=out_specs,
        out_shape=out_shape,
        scratch_shapes=[
            pltpu.VMEM((8, RWKV_COLS_PAD), F32),
            pltpu.VMEM((RWKV_VW // QUAD, QUAD, QUAD), F32),
            pltpu.VMEM((tc, RWKV_VW), F32),
        ],
        compiler_params=pltpu.CompilerParams(
            dimension_semantics=("parallel", "arbitrary"), vmem_limit_bytes=VMEM_LIMIT),
        name="rwkv7",
    )(*args)
    if has_vfirst:
        return res, v_first
    return res[0], res[1]


def _ffn_kernel(x_ref, og_ref, or_ref, wo_g_ref, wo_r_ref, gtm_ref, shf_ref, scf_ref, gtf_ref,
                nfg_ref, wup_ref, cw_ref, cb_ref, wdn_ref, fg_ref, out_ref, tail_ref,
                *, tm, fc, d_ff, final):
    @pl.when(pl.program_id(1) == 0)
    def _():
        tail_ref[...] = jnp.zeros_like(tail_ref)

    mixed = _dot(og_ref[...].astype(BF16), wo_g_ref[...]) + _dot(or_ref[...].astype(BF16), wo_r_ref[...])
    x1 = x_ref[...] + gtm_ref[...] * mixed
    h = _modulated_norm(x1, nfg_ref[...], shf_ref[...], scf_ref[...]).astype(BF16)

    row = lax.broadcasted_iota(jnp.int32, (tm, 1), 0)

    def conv_part(cols):
        u = _dot(h, wup_ref[:, cols])
        tl = tail_ref[:, cols]
        u1 = jnp.where(row == 0, tl[7:8], pltpu.roll(u, 1, 0))
        u2 = jnp.where(row == 0, tl[6:7], jnp.where(row == 1, tl[7:8], pltpu.roll(u, 2, 0)))
        tail_ref[:, cols] = u[tm - 8:tm]
        return cw_ref[0:1, cols] * u2 + cw_ref[1:2, cols] * u1 + cw_ref[2:3, cols] * u + cb_ref[:, cols]

    acc = jnp.zeros((tm, x_ref.shape[-1]), F32)
    for j in range(d_ff // fc):
        gate = conv_part(slice(j * fc, (j + 1) * fc))
        val = conv_part(slice(d_ff + j * fc, d_ff + (j + 1) * fc))
        act = (_silu(gate) * val).astype(BF16)
        acc = acc + _dot(act, wdn_ref[j * fc:(j + 1) * fc, :])
    x2 = x1 + gtf_ref[...] * acc
    if final:
        ms = jnp.mean(x2 * x2, axis=-1, keepdims=True)
        x2 = x2 * lax.rsqrt(ms + NORM_EPS) * fg_ref[...]
    out_ref[...] = x2


def _ffn(x, o_gla, o_rwkv, wo_g, wo_r, mod, nfg, w_up, conv_w, conv_b, w_down, final_g, final, tm, fc):
    B, T, D = x.shape
    d_ff = w_down.shape[0]
    const = lambda b, t: (0, 0)
    tok = lambda w: pl.BlockSpec((None, tm, w), lambda b, t: (b, t, 0))
    modspec = lambda i: pl.BlockSpec((None, None, 1, D), lambda b, t: (i, b, 0, 0))
    single = pl.Buffered(1)
    return pl.pallas_call(
        functools.partial(_ffn_kernel, tm=tm, fc=fc, d_ff=d_ff, final=final),
        grid=(B, T // tm),
        in_specs=[
            tok(D), tok(GLA_VW), tok(RWKV_VW),
            pl.BlockSpec((GLA_VW, D), const, pipeline_mode=single),
            pl.BlockSpec((RWKV_VW, D), const, pipeline_mode=single),
            modspec(2), modspec(3), modspec(4), modspec(5),
            pl.BlockSpec((1, D), const),
            pl.BlockSpec((D, 2 * d_ff), const, pipeline_mode=single),
            pl.BlockSpec((CONV_W, 2 * d_ff), const),
            pl.BlockSpec((1, 2 * d_ff), const),
            pl.BlockSpec((d_ff, D), const, pipeline_mode=single),
            pl.BlockSpec((1, D), const),
        ],
        out_specs=tok(D),
        out_shape=jax.ShapeDtypeStruct((B, T, D), F32),
        scratch_shapes=[pltpu.VMEM((8, 2 * d_ff), F32)],
        compiler_params=pltpu.CompilerParams(
            dimension_semantics=("parallel", "arbitrary"), vmem_limit_bytes=VMEM_LIMIT),
        name="out_ffn",
    )(x, o_gla, o_rwkv, wo_g, wo_r, mod, mod, mod, mod, nfg, w_up, conv_w, conv_b, w_down, final_g)


def _masks(tc):
    i = jnp.arange(tc)
    tril = ((i[:, None] >= i[None, :]) & (i[:, None] // CHUNK == i[None, :] // CHUNK)).astype(BF16)
    hq = jnp.arange(QUAD) // RWKV_HEAD
    bd = (hq[:, None] == hq[None, :]).astype(BF16)
    hv = jnp.arange(GLA_VW) // GLA_DV
    bdv = (hq[:, None] == hv[None, :]).astype(BF16)
    ho = jnp.arange(RWKV_VW) // RWKV_HEAD
    ones_bd = (ho[:, None] == ho[None, :]).astype(BF16)
    ci = jnp.arange(CHUNK)[:, None]
    cj = (jnp.arange(QUAD) % CHUNK)[None, :]
    return dict(tril=tril, bd=bd, bdf=bd.astype(F32), bdv=bdv, bdvt=bdv.T.astype(F32), ones_bd=ones_bd,
                causal=(ci >= cj).astype(F32), strict=(ci > cj).astype(F32), eye=(ci == cj).astype(F32))


def _pad_rows(w, rows, at):
    out = jnp.zeros((rows, w.shape[1]), w.dtype)
    return out.at[at:at + w.shape[0]].set(w)


def kernel(x, c, ada_w, ada_b, norm_mix_g, w_in, w_in_vres, gla_gk_up, gla_gk_b, gla_norm_g,
           rwkv_mu, rwkv_mu_vres, w_lora_up, w0, a_lora_up, a0, g_lora_up, v_lora_up, v0,
           k_k, k_a, r_k, gn_g, gn_b, w_out, norm_ffn_g, ffn_up, ffn_conv_w, ffn_conv_b,
           ffn_down, final_g):
    B, T, D = x.shape
    L = ada_w.shape[0]
    gla_cols = 2 * GLA_KW + 2 * GLA_VW + GLA_GATE_RANK
    tm_in = min(256, T)
    tc = min(256, T)
    tm_ffn = min(512, T)
    fc = 256
    consts = _masks(tc)

    mod_all = _adaln(c, ada_w, ada_b)
    v_first = None
    for l in range(L):
        mod = mod_all[l].reshape(B, N_MOD, 1, D).transpose(1, 0, 2, 3)
        w_gla = jnp.pad(w_in[l][:, :gla_cols], ((0, 0), (0, GLA_COLS_PAD - gla_cols))).astype(BF16)
        w_r = w_in[l][:, gla_cols:]
        mu = rwkv_mu[l]
        if l > 0:
            w_r = jnp.concatenate([w_r, w_in_vres[l - 1]], axis=1)
            mu = jnp.concatenate([mu, rwkv_mu_vres[l - 1]], axis=0)
        w_rwkv = jnp.pad(w_r, ((0, 0), (0, RWKV_COLS_PAD - w_r.shape[1]))).astype(BF16)
        mu = jnp.pad(mu, (0, RWKV_COLS_PAD - mu.shape[0])).reshape(1, RWKV_COLS_PAD)

        pg, pr = _mix_in(x, mod, norm_mix_g[l].reshape(1, D), w_gla, w_rwkv, tm_in)

        o_gla = _gla(pg, _pad_rows(gla_gk_up[l], LANE, 0), gla_gk_b[l].reshape(1, GLA_KW),
                     gla_norm_g[l].reshape(1, GLA_DV), consts, tc)

        v0_l = v0[l - 1] if l > 0 else jnp.zeros((RWKV_VW,), F32)
        vparams = jnp.stack([w0[l], a0[l], v0_l, k_k[l], k_a[l], r_k[l].reshape(-1), gn_g[l], gn_b[l]])
        v_up = _pad_rows(v_lora_up[l - 1], LANE, 0).astype(BF16) if l > 0 else None
        o_rwkv, v_first = _rwkv(
            pr, mu, vparams,
            _pad_rows(w_lora_up[l], LANE, 0).astype(BF16),
            _pad_rows(a_lora_up[l], LANE, DECAY_LORA).astype(BF16),
            g_lora_up[l].astype(BF16), v_up, v_first, consts, tc)

        x = _ffn(x, o_gla, o_rwkv, w_out[l][:GLA_VW].astype(BF16), w_out[l][GLA_VW:].astype(BF16),
                 mod, norm_ffn_g[l].reshape(1, D), ffn_up[l].astype(BF16), ffn_conv_w[l],
                 ffn_conv_b[l].reshape(1, -1), ffn_down[l].astype(BF16), final_g.reshape(1, D),
                 l == L - 1, tm_ffn, fc)
    return x
```

```python
import functools

import jax
import jax.numpy as jnp
from jax import lax
from jax.experimental import pallas as pl
from jax.experimental.pallas import tpu as pltpu

F32 = jnp.float32
BF16 = jnp.bfloat16

GLA_HEADS = 4
GLA_DK = 64
GLA_DV = 128
GLA_KW = GLA_HEADS * GLA_DK
GLA_VW = GLA_HEADS * GLA_DV
GLA_GATE_RANK = 16
GLA_GATE_NORM = 16.0
RWKV_HEAD = 64
RWKV_VW = 512
DECAY_LORA = 64
AAA_LORA = 64
GATE_LORA = 128
MV_LORA = 32
GN_EPS = 64e-5
NORM_EPS = 1e-6
N_MOD = 6
CONV_W = 3

LANE = 128
QUAD = 256
CHUNK = 64
HALF = CHUNK // 2
GLA_COLS_PAD = 2 * GLA_KW + 2 * GLA_VW + LANE
RWKV_COLS_PAD = 3 * RWKV_VW + DECAY_LORA + AAA_LORA + GATE_LORA + LANE
VMEM_LIMIT = 56 * 1024 * 1024


def _dot(a, b):
    return jnp.dot(a, b, preferred_element_type=F32)


def _dot_nt(a, b):
    return lax.dot_general(a, b, (((1,), (1,)), ((), ())), preferred_element_type=F32)


def _dot_tn(a, b):
    return lax.dot_general(a, b, (((0,), (0,)), ((), ())), preferred_element_type=F32)


def _split(x, terms):
    parts = []
    rem = x
    for _ in range(terms):
        p = rem.astype(BF16)
        parts.append(p)
        rem = rem - p.astype(F32)
    return parts


def _dot_exact_rhs(x, w_bf16, terms):
    acc = None
    for p in _split(x, terms):
        d = _dot(p, w_bf16)
        acc = d if acc is None else acc + d
    return acc


def _cumsum_rows(tril_bf16, x, terms=3):
    acc = None
    for p in _split(x, terms):
        d = _dot(tril_bf16, p)
        acc = d if acc is None else acc + d
    return acc


def _silu(x):
    return x * jax.nn.sigmoid(x)


def _softplus(x):
    return jnp.maximum(x, 0.0) + jnp.log1p(jnp.exp(-jnp.abs(x)))


def _tile4(x):
    return jnp.concatenate([x, x, x, x], axis=0)


def _mod_kernel(c_ref, w_ref, b_ref, o_ref):
    s = _silu(c_ref[...])
    o_ref[...] = jnp.dot(s, w_ref[...], precision=lax.Precision.HIGHEST,
                         preferred_element_type=F32) + b_ref[...]


def _adaln(c, ada_w, ada_b):
    L, D, N = ada_w.shape
    B = c.shape[0]
    rows = 8 * ((B + 7) // 8)
    cp = jnp.zeros((rows, D), F32).at[:B].set(c)
    tn = 1536
    assert N % tn == 0
    out = pl.pallas_call(
        _mod_kernel,
        grid=(L, N // tn),
        in_specs=[
            pl.BlockSpec((rows, D), lambda l, n: (0, 0)),
            pl.BlockSpec((None, D, tn), lambda l, n: (l, 0, n)),
            pl.BlockSpec((None, 1, tn), lambda l, n: (l, 0, n)),
        ],
        out_specs=pl.BlockSpec((None, rows, tn), lambda l, n: (l, 0, n)),
        out_shape=jax.ShapeDtypeStruct((L, rows, N), F32),
        compiler_params=pltpu.CompilerParams(
            dimension_semantics=("parallel", "parallel"), vmem_limit_bytes=VMEM_LIMIT),
        name="adaln_mod",
    )(cp, ada_w, ada_b.reshape(L, 1, N))
    return out[:, :B]


def _modulated_norm(x, g, shift, scale):
    ms = jnp.mean(x * x, axis=-1, keepdims=True)
    y = x * lax.rsqrt(ms + NORM_EPS) * g
    return y * (1.0 + scale) + shift


def _mix_in_kernel(x_ref, sh_ref, sc_ref, g_ref, wg_ref, wr_ref, og_ref, or_ref):
    h = _modulated_norm(x_ref[...], g_ref[...], sh_ref[...], sc_ref[...]).astype(BF16)
    og_ref[...] = _dot(h, wg_ref[...])
    or_ref[...] = _dot(h, wr_ref[...])


def _mix_in(x, mod, g, w_gla, w_rwkv, tm):
    B, T, D = x.shape
    const = lambda b, t: (0, 0)
    return pl.pallas_call(
        _mix_in_kernel,
        grid=(B, T // tm),
        in_specs=[
            pl.BlockSpec((None, tm, D), lambda b, t: (b, t, 0)),
            pl.BlockSpec((None, None, 1, D), lambda b, t: (0, b, 0, 0)),
            pl.BlockSpec((None, None, 1, D), lambda b, t: (1, b, 0, 0)),
            pl.BlockSpec((1, D), const),
            pl.BlockSpec((D, GLA_COLS_PAD), const, pipeline_mode=pl.Buffered(1)),
            pl.BlockSpec((D, RWKV_COLS_PAD), const, pipeline_mode=pl.Buffered(1)),
        ],
        out_specs=[
            pl.BlockSpec((None, tm, GLA_COLS_PAD), lambda b, t: (b, t, 0)),
            pl.BlockSpec((None, tm, RWKV_COLS_PAD), lambda b, t: (b, t, 0)),
        ],
        out_shape=[
            jax.ShapeDtypeStruct((B, T, GLA_COLS_PAD), F32),
            jax.ShapeDtypeStruct((B, T, RWKV_COLS_PAD), F32),
        ],
        compiler_params=pltpu.CompilerParams(
            dimension_semantics=("parallel", "parallel"), vmem_limit_bytes=VMEM_LIMIT),
        name="mix_in",
    )(x, mod, mod, g, w_gla, w_rwkv)


def _gla_kernel(p_ref, gkup_ref, gkb_ref, ng_ref, tril_ref, bdk_ref, bdv_ref, bdvt_ref, causal_ref,
                o_ref, st_ref, obuf_ref, *, tc):
    @pl.when(pl.program_id(1) == 0)
    def _():
        st_ref[...] = jnp.zeros_like(st_ref)

    q_all = p_ref[:, 0:GLA_KW] * (GLA_DK ** -0.5)
    k_all = p_ref[:, GLA_KW:2 * GLA_KW]
    z = p_ref[:, 2 * GLA_KW + 2 * GLA_VW:GLA_COLS_PAD]
    pre = jnp.dot(z, gkup_ref[...], precision=lax.Precision.HIGHEST,
                  preferred_element_type=F32) + gkb_ref[...]
    gk = -_softplus(-pre) * (1.0 / GLA_GATE_NORM)
    g_all = _cumsum_rows(tril_ref[...], gk)

    bdk = bdk_ref[...]
    bdv = bdv_ref[...]
    bdvt = bdvt_ref[...]
    causal = causal_ref[...]
    first_half = lax.broadcasted_iota(jnp.int32, (CHUNK, 1), 0) < HALF

    chunks = range(tc // CHUNK)
    rows = lambda c: slice(c * CHUNK, (c + 1) * CHUNK)
    a, upd, qi, vc, decay = {}, {}, {}, {}, {}
    for c in chunks:
        gc = g_all[rows(c)]
        qc = q_all[rows(c)]
        kc = k_all[rows(c)]
        vc[c] = p_ref[rows(c), 2 * GLA_KW:2 * GLA_KW + GLA_VW].astype(BF16)
        g_half = gc[HALF - 1:HALF]
        g_last = gc[CHUNK - 1:CHUNK]
        ref = jnp.where(first_half, 0.0, g_half)
        qp = (qc * jnp.exp(gc - ref)).astype(BF16)
        k0 = jnp.where(first_half, kc * jnp.exp(-gc), 0.0).astype(BF16)
        k1 = (kc * jnp.exp(g_half - gc)).astype(BF16)
        a0 = _dot_nt(qp[0:HALF], _tile4(k0) * bdk)
        a1 = _dot_nt(qp[HALF:CHUNK], _tile4(k1) * bdk)
        a[c] = (jnp.concatenate([a0, a1], axis=0) * causal).astype(BF16)
        kd = (kc * jnp.exp(g_last - gc)).astype(BF16)
        upd[c] = _dot_tn(vc[c], kd) * bdvt
        qi[c] = (qc * jnp.exp(gc)).astype(BF16)
        decay[c] = jnp.exp(g_last)
    st = st_ref[...]
    for c in chunks:
        o = _dot(a[c], _tile4(vc[c]) * bdv) + _dot_nt(qi[c], st.astype(BF16))
        obuf_ref[rows(c), :] = o
        st = st * decay[c] + upd[c]
    st_ref[...] = st

    ng = ng_ref[...]
    for h in range(GLA_HEADS):
        sl = slice(h * GLA_DV, (h + 1) * GLA_DV)
        oh = obuf_ref[:, sl]
        ms = jnp.mean(oh * oh, axis=-1, keepdims=True)
        on = oh * lax.rsqrt(ms + NORM_EPS) * ng
        gate = p_ref[:, 2 * GLA_KW + GLA_VW + h * GLA_DV:2 * GLA_KW + GLA_VW + (h + 1) * GLA_DV]
        o_ref[:, sl] = on * _silu(gate)


def _gla(pg, gk_up_pad, gk_b, norm_g, consts, tc):
    B, T, _ = pg.shape
    const = lambda b, t: (0, 0)
    return pl.pallas_call(
        functools.partial(_gla_kernel, tc=tc),
        grid=(B, T // tc),
        in_specs=[
            pl.BlockSpec((None, tc, GLA_COLS_PAD), lambda b, t: (b, t, 0)),
            pl.BlockSpec((LANE, GLA_KW), const),
            pl.BlockSpec((1, GLA_KW), const),
            pl.BlockSpec((1, GLA_DV), const),
            pl.BlockSpec((tc, tc), const),
            pl.BlockSpec((QUAD, QUAD), const),
            pl.BlockSpec((QUAD, GLA_VW), const),
            pl.BlockSpec((GLA_VW, QUAD), const),
            pl.BlockSpec((CHUNK, QUAD), const),
        ],
        out_specs=pl.BlockSpec((None, tc, GLA_VW), lambda b, t: (b, t, 0)),
        out_shape=jax.ShapeDtypeStruct((B, T, GLA_VW), F32),
        scratch_shapes=[pltpu.VMEM((GLA_VW, GLA_KW), F32), pltpu.VMEM((tc, GLA_VW), F32)],
        compiler_params=pltpu.CompilerParams(
            dimension_semantics=("parallel", "arbitrary"), vmem_limit_bytes=VMEM_LIMIT),
        name="gla",
    )(pg, gk_up_pad, gk_b, norm_g, consts["tril"], consts["bd"], consts["bdv"], consts["bdvt"],
      consts["causal"])


_W0, _A0, _V0, _KK, _KA, _RK, _GNG, _GNB = range(8)


def _rwkv_kernel(*refs, tc, has_vfirst):
    if has_vfirst:
        (p_ref, mu_ref, vp_ref, wup_ref, aup_ref, gup_ref, vup_ref, vf_ref, tril_ref, ones_ref,
         bd_ref, bdf_ref, eye_ref, strict_ref, incl_ref, o_ref, carry_ref, s_ref, ybuf_ref) = refs
    else:
        (p_ref, mu_ref, vp_ref, wup_ref, aup_ref, gup_ref, tril_ref, ones_ref,
         bd_ref, bdf_ref, eye_ref, strict_ref, incl_ref, o_ref, vout_ref, carry_ref, s_ref,
         ybuf_ref) = refs

    @pl.when(pl.program_id(1) == 0)
    def _():
        carry_ref[...] = jnp.zeros_like(carry_ref)
        s_ref[...] = jnp.zeros_like(s_ref)

    p = p_ref[...]
    row = lax.broadcasted_iota(jnp.int32, (tc, 1), 0)
    prev = jnp.where(row == 0, carry_ref[7:8, :], pltpu.roll(p, 1, 0))
    carry_ref[...] = p[tc - 8:tc]
    pm = p + (prev - p) * mu_ref[...]

    V = RWKV_VW
    r = pm[:, 0:V]
    k = pm[:, V:2 * V]
    v = pm[:, 2 * V:3 * V]
    lora_wa = pm[:, 3 * V:3 * V + LANE]
    lora_g = pm[:, 3 * V + LANE:3 * V + 2 * LANE]
    vp = vp_ref[...]

    w_log = -_softplus(-(vp[_W0:_W0 + 1] + _dot(jnp.tanh(lora_wa).astype(BF16), wup_ref[...]))) - 0.5
    lw = -jnp.exp(w_log)
    a_lr = jax.nn.sigmoid(vp[_A0:_A0 + 1] + _dot(lora_wa.astype(BF16), aup_ref[...]))
    g = _dot(jax.nn.sigmoid(lora_g).astype(BF16), gup_ref[...])
    if has_vfirst:
        lora_v = pm[:, 3 * V + 2 * LANE:3 * V + 3 * LANE]
        mix = jax.nn.sigmoid(vp[_V0:_V0 + 1] + _dot(lora_v.astype(BF16), vup_ref[...]))
        v = v + (vf_ref[...] - v) * mix
    else:
        vout_ref[...] = v

    ones_bd = ones_ref[...]
    kk = k * vp[_KK:_KK + 1]
    ss = _dot_exact_rhs(kk * kk, ones_bd, 2)
    kk = kk * lax.rsqrt(jnp.maximum(ss, 1e-24))
    k = k * (1.0 + (a_lr - 1.0) * vp[_KA:_KA + 1])

    cum = _cumsum_rows(tril_ref[...], lw)
    w_inv = jnp.exp(-cum)
    rt = r * jnp.exp(cum)
    kt = (k * w_inv).astype(BF16)
    bt = (kk * a_lr * w_inv).astype(BF16)
    at = (-kk * jnp.exp(cum - lw)).astype(BF16)
    vb = v.astype(BF16)

    bd = bd_ref[...]
    bdf = bdf_ref[...]
    eye = eye_ref[...]
    strict = strict_ref[...]
    incl = incl_ref[...]

    def blockdiag(x):
        return _tile4(x.astype(BF16)) * bd

    n_chunks = tc // CHUNK
    n_quads = V // QUAD
    chains = [(c, q) for c in range(n_chunks) for q in range(n_quads)]
    rows = lambda c: slice(c * CHUNK, (c + 1) * CHUNK)
    lanes = lambda q: slice(q * QUAD, (q + 1) * QUAD)
    sl = {cq: (rows(cq[0]), lanes(cq[1])) for cq in chains}

    pb, pk = {}, {}
    for cq in chains:
        x = jnp.concatenate([at[sl[cq]], rt[sl[cq]].astype(BF16)], axis=0)
        pb[cq] = _dot_nt(x, blockdiag(bt[sl[cq]]))
        pk[cq] = _dot_nt(x, blockdiag(kt[sl[cq]]))
    t, pw, a_rb, av = {}, {}, {}, {}
    for cq in chains:
        a_ab = pb[cq][0:CHUNK] * strict
        a_rb[cq] = (pb[cq][CHUNK:] * incl).astype(BF16)
        a_k = jnp.concatenate([pk[cq][0:CHUNK] * strict, pk[cq][CHUNK:] * incl], axis=0)
        t[cq] = eye + a_ab
        pw[cq] = _dot(a_ab.astype(BF16), blockdiag(a_ab))
        av[cq] = _dot(a_k.astype(BF16), blockdiag(vb[sl[cq]]))
    for _ in range(4):
        for cq in chains:
            tp = _dot(jnp.concatenate([t[cq], pw[cq]], axis=0).astype(BF16), blockdiag(pw[cq]))
            t[cq] = t[cq] + tp[0:CHUNK]
            pw[cq] = tp[CHUNK:]
    for cq in chains:
        t[cq] = (t[cq] + _dot(t[cq].astype(BF16), blockdiag(pw[cq]))).astype(BF16)
    abar, u0 = {}, {}
    for cq in chains:
        abar[cq] = _dot(t[cq], blockdiag(at[sl[cq]]))
        u0[cq] = _dot(t[cq], blockdiag(av[cq][0:CHUNK]))
    rbar, y0, gm, nm = {}, {}, {}, {}
    for cq in chains:
        rbar[cq] = (rt[sl[cq]] + _dot(a_rb[cq], blockdiag(abar[cq]))).astype(BF16)
        y0[cq] = av[cq][CHUNK:] + _dot(a_rb[cq], blockdiag(u0[cq]))
    for cq in chains:
        gm[cq] = (_dot_tn(abar[cq].astype(BF16), bt[sl[cq]]) * bdf).astype(BF16)
        nm[cq] = _dot_tn(jnp.concatenate([u0[cq].astype(BF16), vb[sl[cq]]], axis=0),
                         jnp.concatenate([bt[sl[cq]], kt[sl[cq]]], axis=0)) * bdf
    s = [s_ref[q] for q in range(n_quads)]
    for c in range(n_chunks):
        wc = jnp.exp(cum[(c + 1) * CHUNK - 1:(c + 1) * CHUNK])
        for q in range(n_quads):
            cq = (c, q)
            sb = s[q].astype(BF16)
            ybuf_ref[sl[cq]] = _dot_nt(rbar[cq], sb) + y0[cq]
            s[q] = (s[q] + _dot(sb, gm[cq]) + nm[cq]) * wc[:, lanes(q)]
    for q in range(n_quads):
        s_ref[q] = s[q]

    y = ybuf_ref[...]
    inv_n = 1.0 / RWKV_HEAD
    mean = _dot_exact_rhs(y, ones_bd, 2) * inv_n
    yc = y - mean
    var = _dot_exact_rhs(yc * yc, ones_bd, 2) * inv_n
    yn = yc * lax.rsqrt(var + GN_EPS) * vp[_GNG:_GNG + 1] + vp[_GNB:_GNB + 1]
    bonus = _dot_exact_rhs(r * k * vp[_RK:_RK + 1], ones_bd, 2) * v
    o_ref[...] = (yn + bonus) * g


def _rwkv(pr, mu, vparams, w_up, a_up, g_up, v_up, v_first, consts, tc):
    B, T, _ = pr.shape
    const = lambda b, t: (0, 0)
    has_vfirst = v_first is not None
    tok = lambda w: pl.BlockSpec((None, tc, w), lambda b, t: (b, t, 0))
    in_specs = [
        tok(RWKV_COLS_PAD),
        pl.BlockSpec((1, RWKV_COLS_PAD), const),
        pl.BlockSpec((8, RWKV_VW), const),
        pl.BlockSpec((LANE, RWKV_VW), const),
        pl.BlockSpec((LANE, RWKV_VW), const),
        pl.BlockSpec((LANE, RWKV_VW), const),
    ]
    args = [pr, mu, vparams, w_up, a_up, g_up]
    if has_vfirst:
        in_specs += [pl.BlockSpec((LANE, RWKV_VW), const), tok(RWKV_VW)]
        args += [v_up, v_first]
    in_specs += [
        pl.BlockSpec((tc, tc), const),
        pl.BlockSpec((RWKV_VW, RWKV_VW), const),
        pl.BlockSpec((QUAD, QUAD), const),
        pl.BlockSpec((QUAD, QUAD), const),
        pl.BlockSpec((CHUNK, QUAD), const),
        pl.BlockSpec((CHUNK, QUAD), const),
        pl.BlockSpec((CHUNK, QUAD), const),
    ]
    args += [consts["tril"], consts["ones_bd"], consts["bd"], consts["bdf"], consts["eye"],
             consts["strict"], consts["causal"]]
    out_sds = jax.ShapeDtypeStruct((B, T, RWKV_VW), F32)
    if has_vfirst:
        out_specs, out_shape = tok(RWKV_VW), out_sds
    else:
        out_specs, out_shape = [tok(RWKV_VW), tok(RWKV_VW)], [out_sds, out_sds]
    res = pl.pallas_call(
        functools.partial(_rwkv_kernel, tc=tc, has_vfirst=has_vfirst),
        grid=(B, T // tc),
        in_specs=in_specs,
        out_specs=out_specs,
        out_shape=out_shape,
        scratch_shapes=[
            pltpu.VMEM((8, RWKV_COLS_PAD), F32),
            pltpu.VMEM((RWKV_VW // QUAD, QUAD, QUAD), F32),
            pltpu.VMEM((tc, RWKV_VW), F32),
        ],
        compiler_params=pltpu.CompilerParams(
            dimension_semantics=("parallel", "arbitrary"), vmem_limit_bytes=VMEM_LIMIT),
        name="rwkv7",
    )(*args)
    if has_vfirst:
        return res, v_first
    return res[0], res[1]


def _ffn_kernel(x_ref, og_ref, or_ref, wo_g_ref, wo_r_ref, gtm_ref, shf_ref, scf_ref, gtf_ref,
                nfg_ref, wup_ref, cw_ref, cb_ref, wdn_ref, fg_ref, out_ref, tail_ref, act_ref,
                *, tm, fc, d_ff, final):
    @pl.when(pl.program_id(1) == 0)
    def _():
        tail_ref[...] = jnp.zeros_like(tail_ref)

    mixed = _dot(og_ref[...].astype(BF16), wo_g_ref[...]) + _dot(or_ref[...].astype(BF16), wo_r_ref[...])
    x1 = x_ref[...] + gtm_ref[...] * mixed
    h = _modulated_norm(x1, nfg_ref[...], shf_ref[...], scf_ref[...]).astype(BF16)

    row8 = lax.broadcasted_iota(jnp.int32, (8, 1), 0)

    def shifted(u, tl, k):
        r = pltpu.roll(u, k, 0)
        head = r[0:8]
        for i in range(k):
            head = jnp.where(row8 == i, tl[8 - k + i:8 - k + i + 1], head)
        return jnp.concatenate([head, r[8:]], axis=0)

    def conv(u, cols):
        tl = tail_ref[:, cols]
        tail_ref[:, cols] = u[tm - 8:tm]
        return (cw_ref[0:1, cols] * shifted(u, tl, 2) + cw_ref[1:2, cols] * shifted(u, tl, 1)
                + cw_ref[2:3, cols] * u + cb_ref[:, cols])

    n_fc = d_ff // fc
    gate_cols = lambda j: slice(j * fc, (j + 1) * fc)
    val_cols = lambda j: slice(d_ff + j * fc, d_ff + (j + 1) * fc)
    up = lambda j: (_dot(h, wup_ref[:, gate_cols(j)]), _dot(h, wup_ref[:, val_cols(j)]))

    u_cur = up(0)
    for j in range(n_fc):
        u_nxt = up(j + 1) if j + 1 < n_fc else None
        act = _silu(conv(u_cur[0], gate_cols(j))) * conv(u_cur[1], val_cols(j))
        act_ref[:, gate_cols(j)] = act.astype(BF16)
        u_cur = u_nxt
    x2 = x1 + gtf_ref[...] * _dot(act_ref[...], wdn_ref[...])
    if final:
        ms = jnp.mean(x2 * x2, axis=-1, keepdims=True)
        x2 = x2 * lax.rsqrt(ms + NORM_EPS) * fg_ref[...]
    out_ref[...] = x2


def _ffn(x, o_gla, o_rwkv, wo_g, wo_r, mod, nfg, w_up, conv_w, conv_b, w_down, final_g, final, tm, fc):
    B, T, D = x.shape
    d_ff = w_down.shape[0]
    const = lambda b, t: (0, 0)
    tok = lambda w: pl.BlockSpec((None, tm, w), lambda b, t: (b, t, 0))
    modspec = lambda i: pl.BlockSpec((None, None, 1, D), lambda b, t: (i, b, 0, 0))
    single = pl.Buffered(1)
    return pl.pallas_call(
        functools.partial(_ffn_kernel, tm=tm, fc=fc, d_ff=d_ff, final=final),
        grid=(B, T // tm),
        in_specs=[
            tok(D), tok(GLA_VW), tok(RWKV_VW),
            pl.BlockSpec((GLA_VW, D), const, pipeline_mode=single),
            pl.BlockSpec((RWKV_VW, D), const, pipeline_mode=single),
            modspec(2), modspec(3), modspec(4), modspec(5),
            pl.BlockSpec((1, D), const),
            pl.BlockSpec((D, 2 * d_ff), const, pipeline_mode=single),
            pl.BlockSpec((CONV_W, 2 * d_ff), const),
            pl.BlockSpec((1, 2 * d_ff), const),
            pl.BlockSpec((d_ff, D), const, pipeline_mode=single),
            pl.BlockSpec((1, D), const),
        ],
        out_specs=tok(D),
        out_shape=jax.ShapeDtypeStruct((B, T, D), F32),
        scratch_shapes=[pltpu.VMEM((8, 2 * d_ff), F32), pltpu.VMEM((tm, d_ff), BF16)],
        compiler_params=pltpu.CompilerParams(
            dimension_semantics=("parallel", "arbitrary"), vmem_limit_bytes=VMEM_LIMIT),
        name="out_ffn",
    )(x, o_gla, o_rwkv, wo_g, wo_r, mod, mod, mod, mod, nfg, w_up, conv_w, conv_b, w_down, final_g)


def _masks(tc):
    i = jnp.arange(tc)
    tril = ((i[:, None] >= i[None, :]) & (i[:, None] // CHUNK == i[None, :] // CHUNK)).astype(BF16)
    hq = jnp.arange(QUAD) // RWKV_HEAD
    bd = (hq[:, None] == hq[None, :]).astype(BF16)
    hv = jnp.arange(GLA_VW) // GLA_DV
    bdv = (hq[:, None] == hv[None, :]).astype(BF16)
    ho = jnp.arange(RWKV_VW) // RWKV_HEAD
    ones_bd = (ho[:, None] == ho[None, :]).astype(BF16)
    ci = jnp.arange(CHUNK)[:, None]
    cj = (jnp.arange(QUAD) % CHUNK)[None, :]
    return dict(tril=tril, bd=bd, bdf=bd.astype(F32), bdv=bdv, bdvt=bdv.T.astype(F32), ones_bd=ones_bd,
                causal=(ci >= cj).astype(F32), strict=(ci > cj).astype(F32), eye=(ci == cj).astype(F32))


def _pad_rows(w, rows, at):
    out = jnp.zeros((rows, w.shape[1]), w.dtype)
    return out.at[at:at + w.shape[0]].set(w)


def kernel(x, c, ada_w, ada_b, norm_mix_g, w_in, w_in_vres, gla_gk_up, gla_gk_b, gla_norm_g,
           rwkv_mu, rwkv_mu_vres, w_lora_up, w0, a_lora_up, a0, g_lora_up, v_lora_up, v0,
           k_k, k_a, r_k, gn_g, gn_b, w_out, norm_ffn_g, ffn_up, ffn_conv_w, ffn_conv_b,
           ffn_down, final_g):
    B, T, D = x.shape
    L = ada_w.shape[0]
    gla_cols = 2 * GLA_KW + 2 * GLA_VW + GLA_GATE_RANK
    tm_in = min(256, T)
    tc = min(256, T)
    tm_ffn = min(512, T)
    fc = 256
    consts = _masks(tc)

    mod_all = _adaln(c, ada_w, ada_b)
    v_first = None
    for l in range(L):
        mod = mod_all[l].reshape(B, N_MOD, 1, D).transpose(1, 0, 2, 3)
        w_gla = jnp.pad(w_in[l][:, :gla_cols], ((0, 0), (0, GLA_COLS_PAD - gla_cols))).astype(BF16)
        w_r = w_in[l][:, gla_cols:]
        mu = rwkv_mu[l]
        if l > 0:
            w_r = jnp.concatenate([w_r, w_in_vres[l - 1]], axis=1)
            mu = jnp.concatenate([mu, rwkv_mu_vres[l - 1]], axis=0)
        w_rwkv = jnp.pad(w_r, ((0, 0), (0, RWKV_COLS_PAD - w_r.shape[1]))).astype(BF16)
        mu = jnp.pad(mu, (0, RWKV_COLS_PAD - mu.shape[0])).reshape(1, RWKV_COLS_PAD)

        pg, pr = _mix_in(x, mod, norm_mix_g[l].reshape(1, D), w_gla, w_rwkv, tm_in)

        o_gla = _gla(pg, _pad_rows(gla_gk_up[l], LANE, 0), gla_gk_b[l].reshape(1, GLA_KW),
                     gla_norm_g[l].reshape(1, GLA_DV), consts, tc)

        v0_l = v0[l - 1] if l > 0 else jnp.zeros((RWKV_VW,), F32)
        vparams = jnp.stack([w0[l], a0[l], v0_l, k_k[l], k_a[l], r_k[l].reshape(-1), gn_g[l], gn_b[l]])
        v_up = _pad_rows(v_lora_up[l - 1], LANE, 0).astype(BF16) if l > 0 else None
        o_rwkv, v_first = _rwkv(
            pr, mu, vparams,
            _pad_rows(w_lora_up[l], LANE, 0).astype(BF16),
            _pad_rows(a_lora_up[l], LANE, DECAY_LORA).astype(BF16),
            g_lora_up[l].astype(BF16), v_up, v_first, consts, tc)

        x = _ffn(x, o_gla, o_rwkv, w_out[l][:GLA_VW].astype(BF16), w_out[l][GLA_VW:].astype(BF16),
                 mod, norm_ffn_g[l].reshape(1, D), ffn_up[l].astype(BF16), ffn_conv_w[l],
                 ffn_conv_b[l].reshape(1, -1), ffn_down[l].astype(BF16), final_g.reshape(1, D),
                 l == L - 1, tm_ffn, fc)
    return x
```

```python
import functools

import jax
import jax.numpy as jnp
from jax import lax
from jax.experimental import pallas as pl
from jax.experimental.pallas import tpu as pltpu

F32 = jnp.float32
BF16 = jnp.bfloat16

GLA_HEADS = 4
GLA_DK = 64
GLA_DV = 128
GLA_KW = GLA_HEADS * GLA_DK
GLA_VW = GLA_HEADS * GLA_DV
GLA_GATE_RANK = 16
GLA_GATE_NORM = 16.0
RWKV_HEAD = 64
RWKV_VW = 512
DECAY_LORA = 64
AAA_LORA = 64
GATE_LORA = 128
MV_LORA = 32
GN_EPS = 64e-5
NORM_EPS = 1e-6
N_MOD = 6
CONV_W = 3

LANE = 128
QUAD = 256
CHUNK = 64
HALF = CHUNK // 2
TRIL_ROWS = 256
SUB_ROWS = 256
GLA_COLS_PAD = 2 * GLA_KW + 2 * GLA_VW + LANE
RWKV_COLS_PAD = 3 * RWKV_VW + DECAY_LORA + AAA_LORA + GATE_LORA + LANE
VMEM_LIMIT = 56 * 1024 * 1024


def _dot(a, b):
    return jnp.dot(a, b, preferred_element_type=F32)


def _dot_nt(a, b):
    return lax.dot_general(a, b, (((1,), (1,)), ((), ())), preferred_element_type=F32)


def _dot_tn(a, b):
    return lax.dot_general(a, b, (((0,), (0,)), ((), ())), preferred_element_type=F32)


def _split(x, terms):
    parts = []
    rem = x
    for _ in range(terms):
        p = rem.astype(BF16)
        parts.append(p)
        rem = rem - p.astype(F32)
    return parts


def _head_sums(x, bd_bf16, terms=2):
    parts = _split(x, terms)
    out = []
    for q in range(x.shape[1] // QUAD):
        acc = None
        for p in parts:
            d = _dot(p[:, q * QUAD:(q + 1) * QUAD], bd_bf16)
            acc = d if acc is None else acc + d
        out.append(acc)
    return jnp.concatenate(out, axis=1)


def _cumsum_rows(tril_bf16, x, terms=2):
    blk = tril_bf16.shape[0]
    parts = _split(x, terms)
    out = []
    for r in range(x.shape[0] // blk):
        acc = None
        for p in parts:
            d = _dot(tril_bf16, p[r * blk:(r + 1) * blk])
            acc = d if acc is None else acc + d
        out.append(acc)
    return jnp.concatenate(out, axis=0)


def _silu(x):
    return x * jax.nn.sigmoid(x)


def _softplus(x):
    return jnp.maximum(x, 0.0) + jnp.log(1.0 + jnp.exp(-jnp.abs(x)))


def _tile4(x):
    return jnp.concatenate([x, x, x, x], axis=0)


def _mod_kernel(c_ref, w_ref, b_ref, o_ref):
    s = _silu(c_ref[...])
    o_ref[...] = jnp.dot(s, w_ref[...], precision=lax.Precision.HIGHEST,
                         preferred_element_type=F32) + b_ref[...]


def _adaln(c, ada_w, ada_b):
    L, D, N = ada_w.shape
    B = c.shape[0]
    rows = 8 * ((B + 7) // 8)
    cp = jnp.zeros((rows, D), F32).at[:B].set(c)
    tn = 1536
    assert N % tn == 0
    out = pl.pallas_call(
        _mod_kernel,
        grid=(L, N // tn),
        in_specs=[
            pl.BlockSpec((rows, D), lambda l, n: (0, 0)),
            pl.BlockSpec((None, D, tn), lambda l, n: (l, 0, n)),
            pl.BlockSpec((None, 1, tn), lambda l, n: (l, 0, n)),
        ],
        out_specs=pl.BlockSpec((None, rows, tn), lambda l, n: (l, 0, n)),
        out_shape=jax.ShapeDtypeStruct((L, rows, N), F32),
        compiler_params=pltpu.CompilerParams(
            dimension_semantics=("parallel", "parallel"), vmem_limit_bytes=VMEM_LIMIT),
        name="adaln_mod",
    )(cp, ada_w, ada_b.reshape(L, 1, N))
    return out[:, :B]


def _modulated_norm(x, g, shift, scale):
    ms = jnp.mean(x * x, axis=-1, keepdims=True)
    y = x * lax.rsqrt(ms + NORM_EPS) * g
    return y * (1.0 + scale) + shift


def _mix_in_kernel(x_ref, sh_ref, sc_ref, g_ref, wg_ref, wr_ref, og_ref, or_ref):
    for r0 in range(0, x_ref.shape[0], SUB_ROWS):
        rs = slice(r0, r0 + SUB_ROWS)
        h = _modulated_norm(x_ref[rs, :], g_ref[...], sh_ref[...], sc_ref[...]).astype(BF16)
        og_ref[rs, :] = _dot(h, wg_ref[...])
        or_ref[rs, :] = _dot(h, wr_ref[...])


def _mix_in(x, mod, g, w_gla, w_rwkv, tm):
    B, T, D = x.shape
    const = lambda b, t: (0, 0)
    return pl.pallas_call(
        _mix_in_kernel,
        grid=(B, T // tm),
        in_specs=[
            pl.BlockSpec((None, tm, D), lambda b, t: (b, t, 0)),
            pl.BlockSpec((None, None, 1, D), lambda b, t: (0, b, 0, 0)),
            pl.BlockSpec((None, None, 1, D), lambda b, t: (1, b, 0, 0)),
            pl.BlockSpec((1, D), const),
            pl.BlockSpec((D, GLA_COLS_PAD), const, pipeline_mode=pl.Buffered(1)),
            pl.BlockSpec((D, RWKV_COLS_PAD), const, pipeline_mode=pl.Buffered(1)),
        ],
        out_specs=[
            pl.BlockSpec((None, tm, GLA_COLS_PAD), lambda b, t: (b, t, 0)),
            pl.BlockSpec((None, tm, RWKV_COLS_PAD), lambda b, t: (b, t, 0)),
        ],
        out_shape=[
            jax.ShapeDtypeStruct((B, T, GLA_COLS_PAD), F32),
            jax.ShapeDtypeStruct((B, T, RWKV_COLS_PAD), F32),
        ],
        compiler_params=pltpu.CompilerParams(
            dimension_semantics=("parallel", "parallel"), vmem_limit_bytes=VMEM_LIMIT),
        name="mix_in",
    )(x, mod, mod, g, w_gla, w_rwkv)


def _gla_kernel(p_ref, gkup_ref, gkb_ref, ng_ref, tril_ref, bdk_ref, bdv_ref, bdvt_ref, causal_ref,
                o_ref, st_ref, obuf_ref, *, tc):
    @pl.when(pl.program_id(1) == 0)
    def _():
        st_ref[...] = jnp.zeros_like(st_ref)

    q_all = p_ref[:, 0:GLA_KW] * (GLA_DK ** -0.5)
    k_all = p_ref[:, GLA_KW:2 * GLA_KW]
    z = p_ref[:, 2 * GLA_KW + 2 * GLA_VW:GLA_COLS_PAD]
    pre = jnp.dot(z, gkup_ref[...], precision=lax.Precision.HIGHEST,
                  preferred_element_type=F32) + gkb_ref[...]
    gk = -_softplus(-pre) * (1.0 / GLA_GATE_NORM)
    g_all = _cumsum_rows(tril_ref[...], gk)

    bdk = bdk_ref[...]
    bdv = bdv_ref[...]
    bdvt = bdvt_ref[...]
    causal = causal_ref[...]
    first_half = lax.broadcasted_iota(jnp.int32, (CHUNK, 1), 0) < HALF

    chunks = range(tc // CHUNK)
    rows = lambda c: slice(c * CHUNK, (c + 1) * CHUNK)
    a, upd, qi, vc, decay = {}, {}, {}, {}, {}
    for c in chunks:
        gc = g_all[rows(c)]
        qc = q_all[rows(c)]
        kc = k_all[rows(c)]
        vc[c] = p_ref[rows(c), 2 * GLA_KW:2 * GLA_KW + GLA_VW].astype(BF16)
        g_half = gc[HALF - 1:HALF]
        g_last = gc[CHUNK - 1:CHUNK]
        ref = jnp.where(first_half, 0.0, g_half)
        qp = (qc * jnp.exp(gc - ref)).astype(BF16)
        k0 = jnp.where(first_half, kc * jnp.exp(-gc), 0.0).astype(BF16)
        k1 = (kc * jnp.exp(g_half - gc)).astype(BF16)
        a0 = _dot_nt(qp[0:HALF], _tile4(k0) * bdk)
        a1 = _dot_nt(qp[HALF:CHUNK], _tile4(k1) * bdk)
        a[c] = (jnp.concatenate([a0, a1], axis=0) * causal).astype(BF16)
        kd = (kc * jnp.exp(g_last - gc)).astype(BF16)
        upd[c] = _dot_tn(vc[c], kd) * bdvt
        qi[c] = (qc * jnp.exp(gc)).astype(BF16)
        decay[c] = jnp.exp(g_last)
    st = st_ref[...]
    for c in chunks:
        o = _dot(a[c], _tile4(vc[c]) * bdv) + _dot_nt(qi[c], st.astype(BF16))
        obuf_ref[rows(c), :] = o
        st = st * decay[c] + upd[c]
    st_ref[...] = st

    ng = ng_ref[...]
    for h in range(GLA_HEADS):
        sl = slice(h * GLA_DV, (h + 1) * GLA_DV)
        oh = obuf_ref[:, sl]
        ms = jnp.mean(oh * oh, axis=-1, keepdims=True)
        on = oh * lax.rsqrt(ms + NORM_EPS) * ng
        gate = p_ref[:, 2 * GLA_KW + GLA_VW + h * GLA_DV:2 * GLA_KW + GLA_VW + (h + 1) * GLA_DV]
        o_ref[:, sl] = on * _silu(gate)


def _gla(pg, gk_up_pad, gk_b, norm_g, consts, tc):
    B, T, _ = pg.shape
    const = lambda b, t: (0, 0)
    return pl.pallas_call(
        functools.partial(_gla_kernel, tc=tc),
        grid=(B, T // tc),
        in_specs=[
            pl.BlockSpec((None, tc, GLA_COLS_PAD), lambda b, t: (b, t, 0)),
            pl.BlockSpec((LANE, GLA_KW), const),
            pl.BlockSpec((1, GLA_KW), const),
            pl.BlockSpec((1, GLA_DV), const),
            pl.BlockSpec((TRIL_ROWS, TRIL_ROWS), const),
            pl.BlockSpec((QUAD, QUAD), const),
            pl.BlockSpec((QUAD, GLA_VW), const),
            pl.BlockSpec((GLA_VW, QUAD), const),
            pl.BlockSpec((CHUNK, QUAD), const),
        ],
        out_specs=pl.BlockSpec((None, tc, GLA_VW), lambda b, t: (b, t, 0)),
        out_shape=jax.ShapeDtypeStruct((B, T, GLA_VW), F32),
        scratch_shapes=[pltpu.VMEM((GLA_VW, GLA_KW), F32), pltpu.VMEM((tc, GLA_VW), F32)],
        compiler_params=pltpu.CompilerParams(
            dimension_semantics=("parallel", "arbitrary"), vmem_limit_bytes=VMEM_LIMIT),
        name="gla",
    )(pg, gk_up_pad, gk_b, norm_g, consts["tril"], consts["bd"], consts["bdv"], consts["bdvt"],
      consts["causal"])


_W0, _A0, _V0, _KK, _KA, _RK, _GNG, _GNB = range(8)


def _rwkv_kernel(*refs, tc, has_vfirst):
    if has_vfirst:
        (p_ref, mu_ref, vp_ref, wup_ref, aup_ref, gup_ref, vup_ref, vf_ref, tril_ref,
         bd_ref, bdf_ref, eye_ref, strict_ref, incl_ref, o_ref, carry_ref, s_ref, ybuf_ref) = refs
    else:
        (p_ref, mu_ref, vp_ref, wup_ref, aup_ref, gup_ref, tril_ref,
         bd_ref, bdf_ref, eye_ref, strict_ref, incl_ref, o_ref, vout_ref, carry_ref, s_ref,
         ybuf_ref) = refs

    @pl.when(pl.program_id(1) == 0)
    def _():
        carry_ref[...] = jnp.zeros_like(carry_ref)
        s_ref[...] = jnp.zeros_like(s_ref)

    p = p_ref[...]
    rolled = pltpu.roll(p, 1, 0)
    row8 = lax.broadcasted_iota(jnp.int32, (8, 1), 0)
    prev = jnp.concatenate([jnp.where(row8 == 0, carry_ref[7:8, :], rolled[0:8]), rolled[8:]], axis=0)
    carry_ref[...] = p[tc - 8:tc]
    pm = p + (prev - p) * mu_ref[...]

    V = RWKV_VW
    r = pm[:, 0:V]
    k = pm[:, V:2 * V]
    v = pm[:, 2 * V:3 * V]
    lora_wa = pm[:, 3 * V:3 * V + LANE]
    lora_g = pm[:, 3 * V + LANE:3 * V + 2 * LANE]
    vp = vp_ref[...]

    w_log = -_softplus(-(vp[_W0:_W0 + 1] + _dot(jnp.tanh(lora_wa).astype(BF16), wup_ref[...]))) - 0.5
    lw = -jnp.exp(w_log)
    a_lr = jax.nn.sigmoid(vp[_A0:_A0 + 1] + _dot(lora_wa.astype(BF16), aup_ref[...]))
    g = _dot(jax.nn.sigmoid(lora_g).astype(BF16), gup_ref[...])
    if has_vfirst:
        lora_v = pm[:, 3 * V + 2 * LANE:3 * V + 3 * LANE]
        mix = jax.nn.sigmoid(vp[_V0:_V0 + 1] + _dot(lora_v.astype(BF16), vup_ref[...]))
        v = v + (vf_ref[...] - v) * mix
    else:
        vout_ref[...] = v

    bd = bd_ref[...]
    kk = k * vp[_KK:_KK + 1]
    ss = _head_sums(kk * kk, bd)
    kk = kk * lax.rsqrt(jnp.maximum(ss, 1e-24))
    k = k * (1.0 + (a_lr - 1.0) * vp[_KA:_KA + 1])

    cum = _cumsum_rows(tril_ref[...], lw)
    w_inv = jnp.exp(-cum)
    rt = r * jnp.exp(cum)
    kt = (k * w_inv).astype(BF16)
    bt = (kk * a_lr * w_inv).astype(BF16)
    at = (-kk * jnp.exp(cum - lw)).astype(BF16)
    vb = v.astype(BF16)

    bdf = bdf_ref[...]
    eye = eye_ref[...]
    strict = strict_ref[...]
    incl = incl_ref[...]

    def blockdiag(x):
        return _tile4(x.astype(BF16)) * bd

    n_chunks = tc // CHUNK
    n_quads = V // QUAD
    chains = [(c, q) for c in range(n_chunks) for q in range(n_quads)]
    rows = lambda c: slice(c * CHUNK, (c + 1) * CHUNK)
    lanes = lambda q: slice(q * QUAD, (q + 1) * QUAD)
    sl = {cq: (rows(cq[0]), lanes(cq[1])) for cq in chains}

    pb, pk = {}, {}
    for cq in chains:
        x = jnp.concatenate([at[sl[cq]], rt[sl[cq]].astype(BF16)], axis=0)
        pb[cq] = _dot_nt(x, blockdiag(bt[sl[cq]]))
        pk[cq] = _dot_nt(x, blockdiag(kt[sl[cq]]))
    t, pw, a_rb, av = {}, {}, {}, {}
    for cq in chains:
        a_ab = pb[cq][0:CHUNK] * strict
        a_rb[cq] = (pb[cq][CHUNK:] * incl).astype(BF16)
        a_k = jnp.concatenate([pk[cq][0:CHUNK] * strict, pk[cq][CHUNK:] * incl], axis=0)
        t[cq] = eye + a_ab
        pw[cq] = _dot(a_ab.astype(BF16), blockdiag(a_ab))
        av[cq] = _dot(a_k.astype(BF16), blockdiag(vb[sl[cq]]))
    for _ in range(4):
        for cq in chains:
            tp = _dot(jnp.concatenate([t[cq], pw[cq]], axis=0).astype(BF16), blockdiag(pw[cq]))
            t[cq] = t[cq] + tp[0:CHUNK]
            pw[cq] = tp[CHUNK:]
    for cq in chains:
        t[cq] = (t[cq] + _dot(t[cq].astype(BF16), blockdiag(pw[cq]))).astype(BF16)
    abar, u0 = {}, {}
    for cq in chains:
        abar[cq] = _dot(t[cq], blockdiag(at[sl[cq]]))
        u0[cq] = _dot(t[cq], blockdiag(av[cq][0:CHUNK]))
    rbar, y0, gm, nm = {}, {}, {}, {}
    for cq in chains:
        rbar[cq] = (rt[sl[cq]] + _dot(a_rb[cq], blockdiag(abar[cq]))).astype(BF16)
        y0[cq] = av[cq][CHUNK:] + _dot(a_rb[cq], blockdiag(u0[cq]))
    for cq in chains:
        gm[cq] = (_dot_tn(abar[cq].astype(BF16), bt[sl[cq]]) * bdf).astype(BF16)
        nm[cq] = _dot_tn(jnp.concatenate([u0[cq].astype(BF16), vb[sl[cq]]], axis=0),
                         jnp.concatenate([bt[sl[cq]], kt[sl[cq]]], axis=0)) * bdf
    s = [s_ref[q] for q in range(n_quads)]
    for c in range(n_chunks):
        wc = jnp.exp(cum[(c + 1) * CHUNK - 1:(c + 1) * CHUNK])
        for q in range(n_quads):
            cq = (c, q)
            sb = s[q].astype(BF16)
            ybuf_ref[sl[cq]] = _dot_nt(rbar[cq], sb) + y0[cq]
            s[q] = (s[q] + _dot(sb, gm[cq]) + nm[cq]) * wc[:, lanes(q)]
    for q in range(n_quads):
        s_ref[q] = s[q]

    y = ybuf_ref[...]
    inv_n = 1.0 / RWKV_HEAD
    mean = _head_sums(y, bd) * inv_n
    yc = y - mean
    var = _head_sums(yc * yc, bd) * inv_n
    yn = yc * lax.rsqrt(var + GN_EPS) * vp[_GNG:_GNG + 1] + vp[_GNB:_GNB + 1]
    bonus = _head_sums(r * k * vp[_RK:_RK + 1], bd) * v
    o_ref[...] = (yn + bonus) * g


def _rwkv(pr, mu, vparams, w_up, a_up, g_up, v_up, v_first, consts, tc):
    B, T, _ = pr.shape
    const = lambda b, t: (0, 0)
    has_vfirst = v_first is not None
    tok = lambda w: pl.BlockSpec((None, tc, w), lambda b, t: (b, t, 0))
    in_specs = [
        tok(RWKV_COLS_PAD),
        pl.BlockSpec((1, RWKV_COLS_PAD), const),
        pl.BlockSpec((8, RWKV_VW), const),
        pl.BlockSpec((LANE, RWKV_VW), const),
        pl.BlockSpec((LANE, RWKV_VW), const),
        pl.BlockSpec((LANE, RWKV_VW), const),
    ]
    args = [pr, mu, vparams, w_up, a_up, g_up]
    if has_vfirst:
        in_specs += [pl.BlockSpec((LANE, RWKV_VW), const), tok(RWKV_VW)]
        args += [v_up, v_first]
    in_specs += [
        pl.BlockSpec((TRIL_ROWS, TRIL_ROWS), const),
        pl.BlockSpec((QUAD, QUAD), const),
        pl.BlockSpec((QUAD, QUAD), const),
        pl.BlockSpec((CHUNK, QUAD), const),
        pl.BlockSpec((CHUNK, QUAD), const),
        pl.BlockSpec((CHUNK, QUAD), const),
    ]
    args += [consts["tril"], consts["bd"], consts["bdf"], consts["eye"],
             consts["strict"], consts["causal"]]
    out_sds = jax.ShapeDtypeStruct((B, T, RWKV_VW), F32)
    if has_vfirst:
        out_specs, out_shape = tok(RWKV_VW), out_sds
    else:
        out_specs, out_shape = [tok(RWKV_VW), tok(RWKV_VW)], [out_sds, out_sds]
    res = pl.pallas_call(
        functools.partial(_rwkv_kernel, tc=tc, has_vfirst=has_vfirst),
        grid=(B, T // tc),
        in_specs=in_specs,
        out_specs=out_specs,
        out_shape=out_shape,
        scratch_shapes=[
            pltpu.VMEM((8, RWKV_COLS_PAD), F32),
            pltpu.VMEM((RWKV_VW // QUAD, QUAD, QUAD), F32),
            pltpu.VMEM((tc, RWKV_VW), F32),
        ],
        compiler_params=pltpu.CompilerParams(
            dimension_semantics=("parallel", "arbitrary"), vmem_limit_bytes=VMEM_LIMIT),
        name="rwkv7",
    )(*args)
    if has_vfirst:
        return res, v_first
    return res[0], res[1]


def _ffn_kernel(x_ref, og_ref, or_ref, wo_g_ref, wo_r_ref, gtm_ref, shf_ref, scf_ref, gtf_ref,
                nfg_ref, wup_ref, cw_ref, cb_ref, wdn_ref, fg_ref, out_ref, tail_ref, act_ref,
                *, tm, fc, d_ff, final):
    @pl.when(pl.program_id(1) == 0)
    def _():
        tail_ref[...] = jnp.zeros_like(tail_ref)

    mixed = _dot(og_ref[...].astype(BF16), wo_g_ref[...]) + _dot(or_ref[...].astype(BF16), wo_r_ref[...])
    x1 = x_ref[...] + gtm_ref[...] * mixed
    h = _modulated_norm(x1, nfg_ref[...], shf_ref[...], scf_ref[...]).astype(BF16)

    row8 = lax.broadcasted_iota(jnp.int32, (8, 1), 0)

    def shifted(u, tl, k):
        r = pltpu.roll(u, k, 0)
        head = r[0:8]
        for i in range(k):
            head = jnp.where(row8 == i, tl[8 - k + i:8 - k + i + 1], head)
        return jnp.concatenate([head, r[8:]], axis=0)

    def conv(u, cols):
        tl = tail_ref[:, cols]
        tail_ref[:, cols] = u[tm - 8:tm]
        return (cw_ref[0:1, cols] * shifted(u, tl, 2) + cw_ref[1:2, cols] * shifted(u, tl, 1)
                + cw_ref[2:3, cols] * u + cb_ref[:, cols])

    n_fc = d_ff // fc
    gate_cols = lambda j: slice(j * fc, (j + 1) * fc)
    val_cols = lambda j: slice(d_ff + j * fc, d_ff + (j + 1) * fc)
    up = lambda j: (_dot(h, wup_ref[:, gate_cols(j)]), _dot(h, wup_ref[:, val_cols(j)]))

    u_cur = up(0)
    for j in range(n_fc):
        u_nxt = up(j + 1) if j + 1 < n_fc else None
        act = _silu(conv(u_cur[0], gate_cols(j))) * conv(u_cur[1], val_cols(j))
        act_ref[:, gate_cols(j)] = act.astype(BF16)
        u_cur = u_nxt
    x2 = x1 + gtf_ref[...] * _dot(act_ref[...], wdn_ref[...])
    if final:
        ms = jnp.mean(x2 * x2, axis=-1, keepdims=True)
        x2 = x2 * lax.rsqrt(ms + NORM_EPS) * fg_ref[...]
    out_ref[...] = x2


def _ffn(x, o_gla, o_rwkv, wo_g, wo_r, mod, nfg, w_up, conv_w, conv_b, w_down, final_g, final, tm, fc):
    B, T, D = x.shape
    d_ff = w_down.shape[0]
    const = lambda b, t: (0, 0)
    tok = lambda w: pl.BlockSpec((None, tm, w), lambda b, t: (b, t, 0))
    modspec = lambda i: pl.BlockSpec((None, None, 1, D), lambda b, t: (i, b, 0, 0))
    single = pl.Buffered(1)
    return pl.pallas_call(
        functools.partial(_ffn_kernel, tm=tm, fc=fc, d_ff=d_ff, final=final),
        grid=(B, T // tm),
        in_specs=[
            tok(D), tok(GLA_VW), tok(RWKV_VW),
            pl.BlockSpec((GLA_VW, D), const, pipeline_mode=single),
            pl.BlockSpec((RWKV_VW, D), const, pipeline_mode=single),
            modspec(2), modspec(3), modspec(4), modspec(5),
            pl.BlockSpec((1, D), const),
            pl.BlockSpec((D, 2 * d_ff), const, pipeline_mode=single),
            pl.BlockSpec((CONV_W, 2 * d_ff), const),
            pl.BlockSpec((1, 2 * d_ff), const),
            pl.BlockSpec((d_ff, D), const, pipeline_mode=single),
            pl.BlockSpec((1, D), const),
        ],
        out_specs=tok(D),
        out_shape=jax.ShapeDtypeStruct((B, T, D), F32),
        scratch_shapes=[pltpu.VMEM((8, 2 * d_ff), F32), pltpu.VMEM((tm, d_ff), BF16)],
        compiler_params=pltpu.CompilerParams(
            dimension_semantics=("parallel", "arbitrary"), vmem_limit_bytes=VMEM_LIMIT),
        name="out_ffn",
    )(x, o_gla, o_rwkv, wo_g, wo_r, mod, mod, mod, mod, nfg, w_up, conv_w, conv_b, w_down, final_g)


def _masks():
    i = jnp.arange(TRIL_ROWS)
    tril = ((i[:, None] >= i[None, :]) & (i[:, None] // CHUNK == i[None, :] // CHUNK)).astype(BF16)
    hq = jnp.arange(QUAD) // RWKV_HEAD
    bd = (hq[:, None] == hq[None, :]).astype(BF16)
    hv = jnp.arange(GLA_VW) // GLA_DV
    bdv = (hq[:, None] == hv[None, :]).astype(BF16)
    ci = jnp.arange(CHUNK)[:, None]
    cj = (jnp.arange(QUAD) % CHUNK)[None, :]
    return dict(tril=tril, bd=bd, bdf=bd.astype(F32), bdv=bdv, bdvt=bdv.T.astype(F32),
                causal=(ci >= cj).astype(F32), strict=(ci > cj).astype(F32), eye=(ci == cj).astype(F32))


def _pad_rows(w, rows, at):
    out = jnp.zeros((rows, w.shape[1]), w.dtype)
    return out.at[at:at + w.shape[0]].set(w)


def kernel(x, c, ada_w, ada_b, norm_mix_g, w_in, w_in_vres, gla_gk_up, gla_gk_b, gla_norm_g,
           rwkv_mu, rwkv_mu_vres, w_lora_up, w0, a_lora_up, a0, g_lora_up, v_lora_up, v0,
           k_k, k_a, r_k, gn_g, gn_b, w_out, norm_ffn_g, ffn_up, ffn_conv_w, ffn_conv_b,
           ffn_down, final_g):
    B, T, D = x.shape
    L = ada_w.shape[0]
    gla_cols = 2 * GLA_KW + 2 * GLA_VW + GLA_GATE_RANK
    tm_in = min(512, T)
    tc_gla = min(512, T)
    tc_rwkv = min(256, T)
    tm_ffn = min(512, T)
    fc = 256
    consts = _masks()

    mod_all = _adaln(c, ada_w, ada_b)
    v_first = None
    for l in range(L):
        mod = mod_all[l].reshape(B, N_MOD, 1, D).transpose(1, 0, 2, 3)
        w_gla = jnp.pad(w_in[l][:, :gla_cols], ((0, 0), (0, GLA_COLS_PAD - gla_cols))).astype(BF16)
        w_r = w_in[l][:, gla_cols:]
        mu = rwkv_mu[l]
        if l > 0:
            w_r = jnp.concatenate([w_r, w_in_vres[l - 1]], axis=1)
            mu = jnp.concatenate([mu, rwkv_mu_vres[l - 1]], axis=0)
        w_rwkv = jnp.pad(w_r, ((0, 0), (0, RWKV_COLS_PAD - w_r.shape[1]))).astype(BF16)
        mu = jnp.pad(mu, (0, RWKV_COLS_PAD - mu.shape[0])).reshape(1, RWKV_COLS_PAD)

        pg, pr = _mix_in(x, mod, norm_mix_g[l].reshape(1, D), w_gla, w_rwkv, tm_in)

        o_gla = _gla(pg, _pad_rows(gla_gk_up[l], LANE, 0), gla_gk_b[l].reshape(1, GLA_KW),
                     gla_norm_g[l].reshape(1, GLA_DV), consts, tc_gla)

        v0_l = v0[l - 1] if l > 0 else jnp.zeros((RWKV_VW,), F32)
        vparams = jnp.stack([w0[l], a0[l], v0_l, k_k[l], k_a[l], r_k[l].reshape(-1), gn_g[l], gn_b[l]])
        v_up = _pad_rows(v_lora_up[l - 1], LANE, 0).astype(BF16) if l > 0 else None
        o_rwkv, v_first = _rwkv(
            pr, mu, vparams,
            _pad_rows(w_lora_up[l], LANE, 0).astype(BF16),
            _pad_rows(a_lora_up[l], LANE, DECAY_LORA).astype(BF16),
            g_lora_up[l].astype(BF16), v_up, v_first, consts, tc_rwkv)

        x = _ffn(x, o_gla, o_rwkv, w_out[l][:GLA_VW].astype(BF16), w_out[l][GLA_VW:].astype(BF16),
                 mod, norm_ffn_g[l].reshape(1, D), ffn_up[l].astype(BF16), ffn_conv_w[l],
                 ffn_conv_b[l].reshape(1, -1), ffn_down[l].astype(BF16), final_g.reshape(1, D),
                 l == L - 1, tm_ffn, fc)
    return x
```

```python
import functools

import jax
import jax.numpy as jnp
from jax import lax
from jax.experimental import pallas as pl
from jax.experimental.pallas import tpu as pltpu

F32 = jnp.float32
BF16 = jnp.bfloat16

GLA_HEADS = 4
GLA_DK = 64
GLA_DV = 128
GLA_KW = GLA_HEADS * GLA_DK
GLA_VW = GLA_HEADS * GLA_DV
GLA_GATE_RANK = 16
GLA_GATE_NORM = 16.0
RWKV_HEAD = 64
RWKV_VW = 512
DECAY_LORA = 64
AAA_LORA = 64
GATE_LORA = 128
MV_LORA = 32
GN_EPS = 64e-5
NORM_EPS = 1e-6
N_MOD = 6
CONV_W = 3

LANE = 128
QUAD = 256
CHUNK = 64
HALF = CHUNK // 2
TRIL_ROWS = 256
GLA_COLS_PAD = 2 * GLA_KW + 2 * GLA_VW + LANE
RWKV_COLS_PAD = 3 * RWKV_VW + DECAY_LORA + AAA_LORA + GATE_LORA + LANE
PROJ_COLS = GLA_COLS_PAD + RWKV_COLS_PAD
PROJ_BLOCK = 256
VMEM_LIMIT = 56 * 1024 * 1024


def _dot(a, b):
    return jnp.dot(a, b, preferred_element_type=F32)


def _dot_nt(a, b):
    return lax.dot_general(a, b, (((1,), (1,)), ((), ())), preferred_element_type=F32)


def _dot_tn(a, b):
    return lax.dot_general(a, b, (((0,), (0,)), ((), ())), preferred_element_type=F32)


def _split(x, terms):
    parts = []
    rem = x
    for _ in range(terms):
        p = rem.astype(BF16)
        parts.append(p)
        rem = rem - p.astype(F32)
    return parts


def _head_sums(x, bd_bf16, terms=2):
    parts = _split(x, terms)
    out = []
    for q in range(x.shape[1] // QUAD):
        acc = None
        for p in parts:
            d = _dot(p[:, q * QUAD:(q + 1) * QUAD], bd_bf16)
            acc = d if acc is None else acc + d
        out.append(acc)
    return jnp.concatenate(out, axis=1)


def _cumsum_rows(tril_bf16, x, terms=2):
    blk = tril_bf16.shape[0]
    parts = _split(x, terms)
    out = []
    for r in range(x.shape[0] // blk):
        acc = None
        for p in parts:
            d = _dot(tril_bf16, p[r * blk:(r + 1) * blk])
            acc = d if acc is None else acc + d
        out.append(acc)
    return jnp.concatenate(out, axis=0)


def _silu(x):
    return x * jax.nn.sigmoid(x)


def _softplus(x):
    return jnp.maximum(x, 0.0) + jnp.log(1.0 + jnp.exp(-jnp.abs(x)))


def _tile4(x):
    return jnp.concatenate([x, x, x, x], axis=0)


def _interleave(schedule):
    live = list(schedule)
    while live:
        for entry in tuple(live):
            gen, count = entry
            for _ in range(count):
                try:
                    next(gen)
                except StopIteration:
                    live.remove(entry)
                    break


def _mod_kernel(c_ref, w_ref, b_ref, o_ref):
    s = _silu(c_ref[...])
    o_ref[...] = jnp.dot(s, w_ref[...], precision=lax.Precision.HIGHEST,
                         preferred_element_type=F32) + b_ref[...]


def _adaln(c, ada_w, ada_b):
    L, D, N = ada_w.shape
    B = c.shape[0]
    rows = 8 * ((B + 7) // 8)
    cp = jnp.zeros((rows, D), F32).at[:B].set(c)
    tn = 1536
    assert N % tn == 0
    out = pl.pallas_call(
        _mod_kernel,
        grid=(L, N // tn),
        in_specs=[
            pl.BlockSpec((rows, D), lambda l, n: (0, 0)),
            pl.BlockSpec((None, D, tn), lambda l, n: (l, 0, n)),
            pl.BlockSpec((None, 1, tn), lambda l, n: (l, 0, n)),
        ],
        out_specs=pl.BlockSpec((None, rows, tn), lambda l, n: (l, 0, n)),
        out_shape=jax.ShapeDtypeStruct((L, rows, N), F32),
        compiler_params=pltpu.CompilerParams(
            dimension_semantics=("parallel", "parallel"), vmem_limit_bytes=VMEM_LIMIT),
        name="adaln_mod",
    )(cp, ada_w, ada_b.reshape(L, 1, N))
    return out[:, :B]


def _modulated_norm(x, g, shift, scale):
    ms = jnp.mean(x * x, axis=-1, keepdims=True)
    y = x * lax.rsqrt(ms + NORM_EPS) * g
    return y * (1.0 + scale) + shift


def _proj_stages(x_ref, sh_ref, sc_ref, g_ref, w_ref, dst):
    h = _modulated_norm(x_ref[...], g_ref[...], sh_ref[...], sc_ref[...]).astype(BF16)
    yield
    for n in range(0, PROJ_COLS, PROJ_BLOCK):
        dst[:, n:n + PROJ_BLOCK] = _dot(h, w_ref[:, n:n + PROJ_BLOCK])
        yield


def _gla_stages(p, gkup_ref, gkb_ref, ng_ref, tril_ref, bdk_ref, bdv_ref, bdvt_ref, causal_ref,
                o_ref, st_ref, obuf_ref, tc):
    z = p[:, 2 * GLA_KW + 2 * GLA_VW:GLA_COLS_PAD]
    pre = jnp.dot(z, gkup_ref[...], precision=lax.Precision.HIGHEST,
                  preferred_element_type=F32) + gkb_ref[...]
    yield
    gk = -_softplus(-pre) * (1.0 / GLA_GATE_NORM)
    g_all = _cumsum_rows(tril_ref[...], gk)
    yield
    q_all = p[:, 0:GLA_KW] * (GLA_DK ** -0.5)
    k_all = p[:, GLA_KW:2 * GLA_KW]
    bdk = bdk_ref[...]
    bdv = bdv_ref[...]
    bdvt = bdvt_ref[...]
    causal = causal_ref[...]
    first_half = lax.broadcasted_iota(jnp.int32, (CHUNK, 1), 0) < HALF

    chunks = range(tc // CHUNK)
    rows = lambda c: slice(c * CHUNK, (c + 1) * CHUNK)
    a, upd, qi, vc, decay = {}, {}, {}, {}, {}
    for c in chunks:
        gc = g_all[rows(c)]
        qc = q_all[rows(c)]
        kc = k_all[rows(c)]
        vc[c] = p[rows(c), 2 * GLA_KW:2 * GLA_KW + GLA_VW].astype(BF16)
        g_half = gc[HALF - 1:HALF]
        g_last = gc[CHUNK - 1:CHUNK]
        ref = jnp.where(first_half, 0.0, g_half)
        qp = (qc * jnp.exp(gc - ref)).astype(BF16)
        k0 = jnp.where(first_half, kc * jnp.exp(-gc), 0.0).astype(BF16)
        k1 = (kc * jnp.exp(g_half - gc)).astype(BF16)
        a0 = _dot_nt(qp[0:HALF], _tile4(k0) * bdk)
        a1 = _dot_nt(qp[HALF:CHUNK], _tile4(k1) * bdk)
        a[c] = (jnp.concatenate([a0, a1], axis=0) * causal).astype(BF16)
        kd = (kc * jnp.exp(g_last - gc)).astype(BF16)
        upd[c] = _dot_tn(vc[c], kd) * bdvt
        qi[c] = (qc * jnp.exp(gc)).astype(BF16)
        decay[c] = jnp.exp(g_last)
        if c % 2 == 1:
            yield
    st = st_ref[...]
    for c in chunks:
        o = _dot(a[c], _tile4(vc[c]) * bdv) + _dot_nt(qi[c], st.astype(BF16))
        obuf_ref[rows(c), :] = o
        st = st * decay[c] + upd[c]
    st_ref[...] = st
    yield
    ng = ng_ref[...]
    for h in range(GLA_HEADS):
        sl = slice(h * GLA_DV, (h + 1) * GLA_DV)
        oh = obuf_ref[:, sl]
        ms = jnp.mean(oh * oh, axis=-1, keepdims=True)
        on = oh * lax.rsqrt(ms + NORM_EPS) * ng
        gate = p[:, 2 * GLA_KW + GLA_VW + h * GLA_DV:2 * GLA_KW + GLA_VW + (h + 1) * GLA_DV]
        o_ref[:, sl] = on * _silu(gate)


_W0, _A0, _V0, _KK, _KA, _RK, _GNG, _GNB = range(8)


def _rwkv_stages(p, mu_ref, vp_ref, wup_ref, aup_ref, gup_ref, vup_ref, vf_ref, tril_ref,
                 bd_ref, bdf_ref, eye_ref, strict_ref, incl_ref, o_ref, vout_ref,
                 carry_ref, s_ref, ybuf_ref, tc):
    c0 = GLA_COLS_PAD
    pr = p[:, c0:c0 + RWKV_COLS_PAD]
    rolled = pltpu.roll(pr, 1, 0)
    row8 = lax.broadcasted_iota(jnp.int32, (8, 1), 0)
    prev = jnp.concatenate([jnp.where(row8 == 0, carry_ref[7:8, :], rolled[0:8]), rolled[8:]], axis=0)
    carry_ref[...] = pr[tc - 8:tc]
    pm = pr + (prev - pr) * mu_ref[...]

    V = RWKV_VW
    r = pm[:, 0:V]
    k = pm[:, V:2 * V]
    v = pm[:, 2 * V:3 * V]
    lora_wa = pm[:, 3 * V:3 * V + LANE]
    lora_g = pm[:, 3 * V + LANE:3 * V + 2 * LANE]
    vp = vp_ref[...]
    bd = bd_ref[...]

    w_pre = _dot(jnp.tanh(lora_wa).astype(BF16), wup_ref[...])
    a_pre = _dot(lora_wa.astype(BF16), aup_ref[...])
    g = _dot(jax.nn.sigmoid(lora_g).astype(BF16), gup_ref[...])
    if vf_ref is not None:
        lora_v = pm[:, 3 * V + 2 * LANE:3 * V + 3 * LANE]
        v_pre = _dot(lora_v.astype(BF16), vup_ref[...])
    kk = k * vp[_KK:_KK + 1]
    ss = _head_sums(kk * kk, bd)
    yield
    w_log = -_softplus(-(vp[_W0:_W0 + 1] + w_pre)) - 0.5
    lw = -jnp.exp(w_log)
    cum = _cumsum_rows(tril_ref[...], lw)
    yield
    a_lr = jax.nn.sigmoid(vp[_A0:_A0 + 1] + a_pre)
    if vf_ref is not None:
        v = v + (vf_ref[...] - v) * jax.nn.sigmoid(vp[_V0:_V0 + 1] + v_pre)
    else:
        vout_ref[...] = v
    kk = kk * lax.rsqrt(jnp.maximum(ss, 1e-24))
    k = k * (1.0 + (a_lr - 1.0) * vp[_KA:_KA + 1])
    w_inv = jnp.exp(-cum)
    rt = r * jnp.exp(cum)
    kt = (k * w_inv).astype(BF16)
    bt = (kk * a_lr * w_inv).astype(BF16)
    at = (-kk * jnp.exp(cum - lw)).astype(BF16)
    vb = v.astype(BF16)
    yield

    bdf = bdf_ref[...]
    eye = eye_ref[...]
    strict = strict_ref[...]
    incl = incl_ref[...]

    def blockdiag(x):
        return _tile4(x.astype(BF16)) * bd

    n_chunks = tc // CHUNK
    n_quads = V // QUAD
    chains = [(c, q) for c in range(n_chunks) for q in range(n_quads)]
    rows = lambda c: slice(c * CHUNK, (c + 1) * CHUNK)
    lanes = lambda q: slice(q * QUAD, (q + 1) * QUAD)
    sl = {cq: (rows(cq[0]), lanes(cq[1])) for cq in chains}

    pb, pk = {}, {}
    for cq in chains:
        x = jnp.concatenate([at[sl[cq]], rt[sl[cq]].astype(BF16)], axis=0)
        pb[cq] = _dot_nt(x, blockdiag(bt[sl[cq]]))
        pk[cq] = _dot_nt(x, blockdiag(kt[sl[cq]]))
    yield
    t, pw, a_rb, av = {}, {}, {}, {}
    for cq in chains:
        a_ab = pb[cq][0:CHUNK] * strict
        a_rb[cq] = (pb[cq][CHUNK:] * incl).astype(BF16)
        a_k = jnp.concatenate([pk[cq][0:CHUNK] * strict, pk[cq][CHUNK:] * incl], axis=0)
        t[cq] = eye + a_ab
        pw[cq] = _dot(a_ab.astype(BF16), blockdiag(a_ab))
        av[cq] = _dot(a_k.astype(BF16), blockdiag(vb[sl[cq]]))
    yield
    for _ in range(4):
        for cq in chains:
            tp = _dot(jnp.concatenate([t[cq], pw[cq]], axis=0).astype(BF16), blockdiag(pw[cq]))
            t[cq] = t[cq] + tp[0:CHUNK]
            pw[cq] = tp[CHUNK:]
        yield
    for cq in chains:
        t[cq] = (t[cq] + _dot(t[cq].astype(BF16), blockdiag(pw[cq]))).astype(BF16)
    yield
    abar, u0 = {}, {}
    for cq in chains:
        abar[cq] = _dot(t[cq], blockdiag(at[sl[cq]]))
        u0[cq] = _dot(t[cq], blockdiag(av[cq][0:CHUNK]))
    yield
    rbar, y0, gm, nm = {}, {}, {}, {}
    for cq in chains:
        rbar[cq] = (rt[sl[cq]] + _dot(a_rb[cq], blockdiag(abar[cq]))).astype(BF16)
        y0[cq] = av[cq][CHUNK:] + _dot(a_rb[cq], blockdiag(u0[cq]))
    yield
    for cq in chains:
        gm[cq] = (_dot_tn(abar[cq].astype(BF16), bt[sl[cq]]) * bdf).astype(BF16)
        nm[cq] = _dot_tn(jnp.concatenate([u0[cq].astype(BF16), vb[sl[cq]]], axis=0),
                         jnp.concatenate([bt[sl[cq]], kt[sl[cq]]], axis=0)) * bdf
    yield
    s = [s_ref[q] for q in range(n_quads)]
    for c in range(n_chunks):
        wc = jnp.exp(cum[(c + 1) * CHUNK - 1:(c + 1) * CHUNK])
        for q in range(n_quads):
            cq = (c, q)
            sb = s[q].astype(BF16)
            ybuf_ref[sl[cq]] = _dot_nt(rbar[cq], sb) + y0[cq]
            s[q] = (s[q] + _dot(sb, gm[cq]) + nm[cq]) * wc[:, lanes(q)]
    for q in range(n_quads):
        s_ref[q] = s[q]
    yield
    y = ybuf_ref[...]
    inv_n = 1.0 / RWKV_HEAD
    mean = _head_sums(y, bd) * inv_n
    bonus = _head_sums(r * k * vp[_RK:_RK + 1], bd) * v
    yield
    yc = y - mean
    var = _head_sums(yc * yc, bd) * inv_n
    yield
    yn = yc * lax.rsqrt(var + GN_EPS) * vp[_GNG:_GNG + 1] + vp[_GNB:_GNB + 1]
    o_ref[...] = (yn + bonus) * g


def _mixer_kernel(*refs, tc, has_vfirst):
    (x0_ref, xn_ref, sh_ref, sc_ref, g_ref, w_ref,
     gkup_ref, gkb_ref, ng_ref, mu_ref, vp_ref, wup_ref, aup_ref, gup_ref) = refs[:14]
    rest = list(refs[14:])
    vup_ref, vf_ref = (rest.pop(0), rest.pop(0)) if has_vfirst else (None, None)
    (tril_ref, bd_ref, bdf_ref, bdv_ref, bdvt_ref, eye_ref, strict_ref, causal_ref) = rest[:8]
    rest = rest[8:]
    og_ref, or_ref = rest[0], rest[1]
    rest = rest[2:]
    vout_ref = None if has_vfirst else rest.pop(0)
    proj_ref, st_ref, obuf_ref, carry_ref, s_ref, ybuf_ref = rest

    t = pl.program_id(1)
    cur = lax.rem(t, 2)

    @pl.when(t == 0)
    def _():
        st_ref[...] = jnp.zeros_like(st_ref)
        carry_ref[...] = jnp.zeros_like(carry_ref)
        s_ref[...] = jnp.zeros_like(s_ref)
        for _ in _proj_stages(x0_ref, sh_ref, sc_ref, g_ref, w_ref, proj_ref.at[0]):
            pass

    p = proj_ref.at[cur]
    _interleave([
        (_rwkv_stages(p, mu_ref, vp_ref, wup_ref, aup_ref, gup_ref, vup_ref, vf_ref, tril_ref,
                      bd_ref, bdf_ref, eye_ref, strict_ref, causal_ref, or_ref, vout_ref,
                      carry_ref, s_ref, ybuf_ref, tc), 1),
        (_proj_stages(xn_ref, sh_ref, sc_ref, g_ref, w_ref, proj_ref.at[1 - cur]), 3),
        (_gla_stages(p, gkup_ref, gkb_ref, ng_ref, tril_ref, bd_ref, bdv_ref, bdvt_ref, causal_ref,
                     og_ref, st_ref, obuf_ref, tc), 1),
    ])


def _mixer(x, mod, g, w_proj, gk_up_pad, gk_b, norm_g, mu, vparams, w_up, a_up, g_up, v_up,
           v_first, consts, tc):
    B, T, D = x.shape
    nt = T // tc
    const = lambda b, t: (0, 0)
    has_vfirst = v_first is not None
    tok = lambda w: pl.BlockSpec((None, tc, w), lambda b, t: (b, t, 0))
    single = pl.Buffered(1)
    in_specs = [
        pl.BlockSpec((None, tc, D), lambda b, t: (b, 0, 0)),
        pl.BlockSpec((None, tc, D), lambda b, t: (b, jnp.minimum(t + 1, nt - 1), 0)),
        pl.BlockSpec((None, None, 1, D), lambda b, t: (0, b, 0, 0)),
        pl.BlockSpec((None, None, 1, D), lambda b, t: (1, b, 0, 0)),
        pl.BlockSpec((1, D), const),
        pl.BlockSpec((D, PROJ_COLS), const, pipeline_mode=single),
        pl.BlockSpec((LANE, GLA_KW), const),
        pl.BlockSpec((1, GLA_KW), const),
        pl.BlockSpec((1, GLA_DV), const),
        pl.BlockSpec((1, RWKV_COLS_PAD), const),
        pl.BlockSpec((8, RWKV_VW), const),
        pl.BlockSpec((LANE, RWKV_VW), const),
        pl.BlockSpec((LANE, RWKV_VW), const),
        pl.BlockSpec((LANE, RWKV_VW), const),
    ]
    args = [x, x, mod, mod, g, w_proj, gk_up_pad, gk_b, norm_g, mu, vparams, w_up, a_up, g_up]
    if has_vfirst:
        in_specs += [pl.BlockSpec((LANE, RWKV_VW), const), tok(RWKV_VW)]
        args += [v_up, v_first]
    in_specs += [
        pl.BlockSpec((TRIL_ROWS, TRIL_ROWS), const),
        pl.BlockSpec((QUAD, QUAD), const),
        pl.BlockSpec((QUAD, QUAD), const),
        pl.BlockSpec((QUAD, GLA_VW), const),
        pl.BlockSpec((GLA_VW, QUAD), const),
        pl.BlockSpec((CHUNK, QUAD), const),
        pl.BlockSpec((CHUNK, QUAD), const),
        pl.BlockSpec((CHUNK, QUAD), const),
    ]
    args += [consts["tril"], consts["bd"], consts["bdf"], consts["bdv"], consts["bdvt"],
             consts["eye"], consts["strict"], consts["causal"]]
    out_sds = jax.ShapeDtypeStruct((B, T, RWKV_VW), F32)
    n_out = 2 if has_vfirst else 3
    res = pl.pallas_call(
        functools.partial(_mixer_kernel, tc=tc, has_vfirst=has_vfirst),
        grid=(B, nt),
        in_specs=in_specs,
        out_specs=[tok(RWKV_VW)] * n_out,
        out_shape=[out_sds] * n_out,
        scratch_shapes=[
            pltpu.VMEM((2, tc, PROJ_COLS), F32),
            pltpu.VMEM((GLA_VW, GLA_KW), F32),
            pltpu.VMEM((tc, GLA_VW), F32),
            pltpu.VMEM((8, RWKV_COLS_PAD), F32),
            pltpu.VMEM((RWKV_VW // QUAD, QUAD, QUAD), F32),
            pltpu.VMEM((tc, RWKV_VW), F32),
        ],
        compiler_params=pltpu.CompilerParams(
            dimension_semantics=("parallel", "arbitrary"), vmem_limit_bytes=VMEM_LIMIT),
        name="mixer",
    )(*args)
    if has_vfirst:
        return res[0], res[1], v_first
    return res[0], res[1], res[2]


def _ffn_kernel(x_ref, og_ref, or_ref, wo_g_ref, wo_r_ref, gtm_ref, shf_ref, scf_ref, gtf_ref,
                nfg_ref, wup_ref, cw_ref, cb_ref, wdn_ref, fg_ref, out_ref, tail_ref, act_ref,
                *, tm, fc, d_ff, final):
    @pl.when(pl.program_id(1) == 0)
    def _():
        tail_ref[...] = jnp.zeros_like(tail_ref)

    mixed = _dot(og_ref[...].astype(BF16), wo_g_ref[...]) + _dot(or_ref[...].astype(BF16), wo_r_ref[...])
    x1 = x_ref[...] + gtm_ref[...] * mixed
    h = _modulated_norm(x1, nfg_ref[...], shf_ref[...], scf_ref[...]).astype(BF16)

    row8 = lax.broadcasted_iota(jnp.int32, (8, 1), 0)

    def shifted(u, tl, k):
        r = pltpu.roll(u, k, 0)
        head = r[0:8]
        for i in range(k):
            head = jnp.where(row8 == i, tl[8 - k + i:8 - k + i + 1], head)
        return jnp.concatenate([head, r[8:]], axis=0)

    def conv(u, cols):
        tl = tail_ref[:, cols]
        tail_ref[:, cols] = u[tm - 8:tm]
        return (cw_ref[0:1, cols] * shifted(u, tl, 2) + cw_ref[1:2, cols] * shifted(u, tl, 1)
                + cw_ref[2:3, cols] * u + cb_ref[:, cols])

    n_fc = d_ff // fc
    gate_cols = lambda j: slice(j * fc, (j + 1) * fc)
    val_cols = lambda j: slice(d_ff + j * fc, d_ff + (j + 1) * fc)
    up = lambda j: (_dot(h, wup_ref[:, gate_cols(j)]), _dot(h, wup_ref[:, val_cols(j)]))

    u_cur = up(0)
    for j in range(n_fc):
        u_nxt = up(j + 1) if j + 1 < n_fc else None
        act = _silu(conv(u_cur[0], gate_cols(j))) * conv(u_cur[1], val_cols(j))
        act_ref[:, gate_cols(j)] = act.astype(BF16)
        u_cur = u_nxt
    x2 = x1 + gtf_ref[...] * _dot(act_ref[...], wdn_ref[...])
    if final:
        ms = jnp.mean(x2 * x2, axis=-1, keepdims=True)
        x2 = x2 * lax.rsqrt(ms + NORM_EPS) * fg_ref[...]
    out_ref[...] = x2


def _ffn(x, o_gla, o_rwkv, wo_g, wo_r, mod, nfg, w_up, conv_w, conv_b, w_down, final_g, final, tm, fc):
    B, T, D = x.shape
    d_ff = w_down.shape[0]
    const = lambda b, t: (0, 0)
    tok = lambda w: pl.BlockSpec((None, tm, w), lambda b, t: (b, t, 0))
    modspec = lambda i: pl.BlockSpec((None, None, 1, D), lambda b, t: (i, b, 0, 0))
    single = pl.Buffered(1)
    return pl.pallas_call(
        functools.partial(_ffn_kernel, tm=tm, fc=fc, d_ff=d_ff, final=final),
        grid=(B, T // tm),
        in_specs=[
            tok(D), tok(GLA_VW), tok(RWKV_VW),
            pl.BlockSpec((GLA_VW, D), const, pipeline_mode=single),
            pl.BlockSpec((RWKV_VW, D), const, pipeline_mode=single),
            modspec(2), modspec(3), modspec(4), modspec(5),
            pl.BlockSpec((1, D), const),
            pl.BlockSpec((D, 2 * d_ff), const, pipeline_mode=single),
            pl.BlockSpec((CONV_W, 2 * d_ff), const),
            pl.BlockSpec((1, 2 * d_ff), const),
            pl.BlockSpec((d_ff, D), const, pipeline_mode=single),
            pl.BlockSpec((1, D), const),
        ],
        out_specs=tok(D),
        out_shape=jax.ShapeDtypeStruct((B, T, D), F32),
        scratch_shapes=[pltpu.VMEM((8, 2 * d_ff), F32), pltpu.VMEM((tm, d_ff), BF16)],
        compiler_params=pltpu.CompilerParams(
            dimension_semantics=("parallel", "arbitrary"), vmem_limit_bytes=VMEM_LIMIT),
        name="out_ffn",
    )(x, o_gla, o_rwkv, wo_g, wo_r, mod, mod, mod, mod, nfg, w_up, conv_w, conv_b, w_down, final_g)


def _masks():
    i = jnp.arange(TRIL_ROWS)
    tril = ((i[:, None] >= i[None, :]) & (i[:, None] // CHUNK == i[None, :] // CHUNK)).astype(BF16)
    hq = jnp.arange(QUAD) // RWKV_HEAD
    bd = (hq[:, None] == hq[None, :]).astype(BF16)
    hv = jnp.arange(GLA_VW) // GLA_DV
    bdv = (hq[:, None] == hv[None, :]).astype(BF16)
    ci = jnp.arange(CHUNK)[:, None]
    cj = (jnp.arange(QUAD) % CHUNK)[None, :]
    return dict(tril=tril, bd=bd, bdf=bd.astype(F32), bdv=bdv, bdvt=bdv.T.astype(F32),
                causal=(ci >= cj).astype(F32), strict=(ci > cj).astype(F32), eye=(ci == cj).astype(F32))


def _pad_rows(w, rows, at):
    out = jnp.zeros((rows, w.shape[1]), w.dtype)
    return out.at[at:at + w.shape[0]].set(w)


def kernel(x, c, ada_w, ada_b, norm_mix_g, w_in, w_in_vres, gla_gk_up, gla_gk_b, gla_norm_g,
           rwkv_mu, rwkv_mu_vres, w_lora_up, w0, a_lora_up, a0, g_lora_up, v_lora_up, v0,
           k_k, k_a, r_k, gn_g, gn_b, w_out, norm_ffn_g, ffn_up, ffn_conv_w, ffn_conv_b,
           ffn_down, final_g):
    B, T, D = x.shape
    L = ada_w.shape[0]
    gla_cols = 2 * GLA_KW + 2 * GLA_VW + GLA_GATE_RANK
    tc = min(256, T)
    tm_ffn = min(512, T)
    fc = 256
    consts = _masks()

    mod_all = _adaln(c, ada_w, ada_b)
    v_first = None
    for l in range(L):
        mod = mod_all[l].reshape(B, N_MOD, 1, D).transpose(1, 0, 2, 3)
        w_r = w_in[l][:, gla_cols:]
        mu = rwkv_mu[l]
        if l > 0:
            w_r = jnp.concatenate([w_r, w_in_vres[l - 1]], axis=1)
            mu = jnp.concatenate([mu, rwkv_mu_vres[l - 1]], axis=0)
        w_proj = jnp.concatenate([
            jnp.pad(w_in[l][:, :gla_cols], ((0, 0), (0, GLA_COLS_PAD - gla_cols))),
            jnp.pad(w_r, ((0, 0), (0, RWKV_COLS_PAD - w_r.shape[1])))], axis=1).astype(BF16)
        mu = jnp.pad(mu, (0, RWKV_COLS_PAD - mu.shape[0])).reshape(1, RWKV_COLS_PAD)

        v0_l = v0[l - 1] if l > 0 else jnp.zeros((RWKV_VW,), F32)
        vparams = jnp.stack([w0[l], a0[l], v0_l, k_k[l], k_a[l], r_k[l].reshape(-1), gn_g[l], gn_b[l]])
        v_up = _pad_rows(v_lora_up[l - 1], LANE, 0).astype(BF16) if l > 0 else None
        o_gla, o_rwkv, v_first = _mixer(
            x, mod, norm_mix_g[l].reshape(1, D), w_proj,
            _pad_rows(gla_gk_up[l], LANE, 0), gla_gk_b[l].reshape(1, GLA_KW),
            gla_norm_g[l].reshape(1, GLA_DV), mu, vparams,
            _pad_rows(w_lora_up[l], LANE, 0).astype(BF16),
            _pad_rows(a_lora_up[l], LANE, DECAY_LORA).astype(BF16),
            g_lora_up[l].astype(BF16), v_up, v_first, consts, tc)

        x = _ffn(x, o_gla, o_rwkv, w_out[l][:GLA_VW].astype(BF16), w_out[l][GLA_VW:].astype(BF16),
                 mod, norm_ffn_g[l].reshape(1, D), ffn_up[l].astype(BF16), ffn_conv_w[l],
                 ffn_conv_b[l].reshape(1, -1), ffn_down[l].astype(BF16), final_g.reshape(1, D),
                 l == L - 1, tm_ffn, fc)
    return x
```

```python
import functools

import jax
import jax.numpy as jnp
from jax import lax
from jax.experimental import pallas as pl
from jax.experimental.pallas import tpu as pltpu

F32 = jnp.float32
BF16 = jnp.bfloat16

GLA_HEADS = 4
GLA_DK = 64
GLA_DV = 128
GLA_KW = GLA_HEADS * GLA_DK
GLA_VW = GLA_HEADS * GLA_DV
GLA_GATE_RANK = 16
GLA_GATE_NORM = 16.0
RWKV_HEAD = 64
RWKV_VW = 512
DECAY_LORA = 64
AAA_LORA = 64
GATE_LORA = 128
MV_LORA = 32
GN_EPS = 64e-5
NORM_EPS = 1e-6
N_MOD = 6
CONV_W = 3

LANE = 128
QUAD = 256
CHUNK = 64
HALF = CHUNK // 2
TRIL_ROWS = 256
GLA_COLS_PAD = 2 * GLA_KW + 2 * GLA_VW + LANE
RWKV_COLS_PAD = 3 * RWKV_VW + DECAY_LORA + AAA_LORA + GATE_LORA + LANE
PROJ_COLS = GLA_COLS_PAD + RWKV_COLS_PAD
PROJ_BLOCK = 256
VMEM_LIMIT = 56 * 1024 * 1024


def _dot(a, b):
    return jnp.dot(a, b, preferred_element_type=F32)


def _dot_nt(a, b):
    return lax.dot_general(a, b, (((1,), (1,)), ((), ())), preferred_element_type=F32)


def _dot_tn(a, b):
    return lax.dot_general(a, b, (((0,), (0,)), ((), ())), preferred_element_type=F32)


def _split(x, terms):
    parts = []
    rem = x
    for _ in range(terms):
        p = rem.astype(BF16)
        parts.append(p)
        rem = rem - p.astype(F32)
    return parts


def _head_sums(x, bd_bf16, terms=2):
    parts = _split(x, terms)
    out = []
    for q in range(x.shape[1] // QUAD):
        acc = None
        for p in parts:
            d = _dot(p[:, q * QUAD:(q + 1) * QUAD], bd_bf16)
            acc = d if acc is None else acc + d
        out.append(acc)
    return jnp.concatenate(out, axis=1)


def _cumsum_rows(tril_bf16, x, terms=2):
    blk = tril_bf16.shape[0]
    parts = _split(x, terms)
    out = []
    for r in range(x.shape[0] // blk):
        acc = None
        for p in parts:
            d = _dot(tril_bf16, p[r * blk:(r + 1) * blk])
            acc = d if acc is None else acc + d
        out.append(acc)
    return jnp.concatenate(out, axis=0)


def _silu(x):
    return x * jax.nn.sigmoid(x)


def _softplus(x):
    return jnp.maximum(x, 0.0) + jnp.log(1.0 + jnp.exp(-jnp.abs(x)))


def _tile4(x):
    return jnp.concatenate([x, x, x, x], axis=0)


def _interleave(schedule):
    live = list(schedule)
    while live:
        for entry in tuple(live):
            gen, count = entry
            for _ in range(count):
                try:
                    next(gen)
                except StopIteration:
                    live.remove(entry)
                    break


def _mod_kernel(c_ref, w_ref, b_ref, o_ref):
    s = _silu(c_ref[...])
    o_ref[...] = jnp.dot(s, w_ref[...], precision=lax.Precision.HIGHEST,
                         preferred_element_type=F32) + b_ref[...]


def _adaln(c, ada_w, ada_b):
    L, D, N = ada_w.shape
    B = c.shape[0]
    rows = 8 * ((B + 7) // 8)
    cp = jnp.zeros((rows, D), F32).at[:B].set(c)
    tn = 1536
    assert N % tn == 0
    out = pl.pallas_call(
        _mod_kernel,
        grid=(L, N // tn),
        in_specs=[
            pl.BlockSpec((rows, D), lambda l, n: (0, 0)),
            pl.BlockSpec((None, D, tn), lambda l, n: (l, 0, n)),
            pl.BlockSpec((None, 1, tn), lambda l, n: (l, 0, n)),
        ],
        out_specs=pl.BlockSpec((None, rows, tn), lambda l, n: (l, 0, n)),
        out_shape=jax.ShapeDtypeStruct((L, rows, N), F32),
        compiler_params=pltpu.CompilerParams(
            dimension_semantics=("parallel", "parallel"), vmem_limit_bytes=VMEM_LIMIT),
        name="adaln_mod",
    )(cp, ada_w, ada_b.reshape(L, 1, N))
    return out[:, :B]


def _modulated_norm(x, g, shift, scale):
    ms = jnp.mean(x * x, axis=-1, keepdims=True)
    y = x * lax.rsqrt(ms + NORM_EPS) * g
    return y * (1.0 + scale) + shift


def _proj_stages(x_ref, sh_ref, sc_ref, g_ref, w_ref, dst):
    h = _modulated_norm(x_ref[...], g_ref[...], sh_ref[...], sc_ref[...]).astype(BF16)
    yield
    for n in range(0, PROJ_COLS, PROJ_BLOCK):
        dst[:, n:n + PROJ_BLOCK] = _dot(h, w_ref[:, n:n + PROJ_BLOCK])
        yield


def _gla_stages(p, gkup_ref, gkb_ref, ng_ref, tril_ref, bdk_ref, bdv_ref, bdvt_ref, causal_ref,
                o_ref, st_ref, obuf_ref, tc):
    z = p[:, 2 * GLA_KW + 2 * GLA_VW:GLA_COLS_PAD]
    pre = jnp.dot(z, gkup_ref[...], precision=lax.Precision.HIGHEST,
                  preferred_element_type=F32) + gkb_ref[...]
    yield
    gk = -_softplus(-pre) * (1.0 / GLA_GATE_NORM)
    g_all = _cumsum_rows(tril_ref[...], gk)
    yield
    q_all = p[:, 0:GLA_KW] * (GLA_DK ** -0.5)
    k_all = p[:, GLA_KW:2 * GLA_KW]
    bdk = bdk_ref[...]
    bdv = bdv_ref[...]
    bdvt = bdvt_ref[...]
    causal = causal_ref[...]
    first_half = lax.broadcasted_iota(jnp.int32, (CHUNK, 1), 0) < HALF

    chunks = range(tc // CHUNK)
    rows = lambda c: slice(c * CHUNK, (c + 1) * CHUNK)
    a, upd, qi, vc, decay = {}, {}, {}, {}, {}
    for c in chunks:
        gc = g_all[rows(c)]
        qc = q_all[rows(c)]
        kc = k_all[rows(c)]
        vc[c] = p[rows(c), 2 * GLA_KW:2 * GLA_KW + GLA_VW].astype(BF16)
        g_half = gc[HALF - 1:HALF]
        g_last = gc[CHUNK - 1:CHUNK]
        ref = jnp.where(first_half, 0.0, g_half)
        qp = (qc * jnp.exp(gc - ref)).astype(BF16)
        k0 = jnp.where(first_half, kc * jnp.exp(-gc), 0.0).astype(BF16)
        k1 = (kc * jnp.exp(g_half - gc)).astype(BF16)
        a0 = _dot_nt(qp[0:HALF], _tile4(k0) * bdk)
        a1 = _dot_nt(qp[HALF:CHUNK], _tile4(k1) * bdk)
        a[c] = (jnp.concatenate([a0, a1], axis=0) * causal).astype(BF16)
        kd = (kc * jnp.exp(g_last - gc)).astype(BF16)
        upd[c] = _dot_tn(vc[c], kd) * bdvt
        qi[c] = (qc * jnp.exp(gc)).astype(BF16)
        decay[c] = jnp.exp(g_last)
        if c % 2 == 1:
            yield
    st = st_ref[...]
    for c in chunks:
        o = _dot(a[c], _tile4(vc[c]) * bdv) + _dot_nt(qi[c], st.astype(BF16))
        obuf_ref[rows(c), :] = o
        st = st * decay[c] + upd[c]
    st_ref[...] = st
    yield
    ng = ng_ref[...]
    for h in range(GLA_HEADS):
        sl = slice(h * GLA_DV, (h + 1) * GLA_DV)
        oh = obuf_ref[:, sl]
        ms = jnp.mean(oh * oh, axis=-1, keepdims=True)
        on = oh * lax.rsqrt(ms + NORM_EPS) * ng
        gate = p[:, 2 * GLA_KW + GLA_VW + h * GLA_DV:2 * GLA_KW + GLA_VW + (h + 1) * GLA_DV]
        o_ref[:, sl] = on * _silu(gate)


_W0, _A0, _V0, _KK, _KA, _RK, _GNG, _GNB = range(8)


def _rwkv_stages(p, mu_ref, vp_ref, wup_ref, aup_ref, gup_ref, vup_ref, vf_ref, tril_ref,
                 bd_ref, bdf_ref, eye_ref, strict_ref, incl_ref, o_ref, vout_ref,
                 carry_ref, s_ref, ybuf_ref, tc):
    c0 = GLA_COLS_PAD
    pr = p[:, c0:c0 + RWKV_COLS_PAD]
    rolled = pltpu.roll(pr, 1, 0)
    row8 = lax.broadcasted_iota(jnp.int32, (8, 1), 0)
    prev = jnp.concatenate([jnp.where(row8 == 0, carry_ref[7:8, :], rolled[0:8]), rolled[8:]], axis=0)
    carry_ref[...] = pr[tc - 8:tc]
    pm = pr + (prev - pr) * mu_ref[...]

    V = RWKV_VW
    r = pm[:, 0:V]
    k = pm[:, V:2 * V]
    v = pm[:, 2 * V:3 * V]
    lora_wa = pm[:, 3 * V:3 * V + LANE]
    lora_g = pm[:, 3 * V + LANE:3 * V + 2 * LANE]
    vp = vp_ref[...]
    bd = bd_ref[...]

    w_pre = _dot(jnp.tanh(lora_wa).astype(BF16), wup_ref[...])
    a_pre = _dot(lora_wa.astype(BF16), aup_ref[...])
    g = _dot(jax.nn.sigmoid(lora_g).astype(BF16), gup_ref[...])
    if vf_ref is not None:
        lora_v = pm[:, 3 * V + 2 * LANE:3 * V + 3 * LANE]
        v_pre = _dot(lora_v.astype(BF16), vup_ref[...])
    kk = k * vp[_KK:_KK + 1]
    ss = _head_sums(kk * kk, bd)
    yield
    w_log = -_softplus(-(vp[_W0:_W0 + 1] + w_pre)) - 0.5
    lw = -jnp.exp(w_log)
    cum = _cumsum_rows(tril_ref[...], lw)
    yield
    a_lr = jax.nn.sigmoid(vp[_A0:_A0 + 1] + a_pre)
    if vf_ref is not None:
        v = v + (vf_ref[...] - v) * jax.nn.sigmoid(vp[_V0:_V0 + 1] + v_pre)
    else:
        vout_ref[...] = v
    kk = kk * lax.rsqrt(jnp.maximum(ss, 1e-24))
    k = k * (1.0 + (a_lr - 1.0) * vp[_KA:_KA + 1])
    w_inv = jnp.exp(-cum)
    rt = r * jnp.exp(cum)
    kt = (k * w_inv).astype(BF16)
    bt = (kk * a_lr * w_inv).astype(BF16)
    at = (-kk * jnp.exp(cum - lw)).astype(BF16)
    vb = v.astype(BF16)
    yield

    bdf = bdf_ref[...]
    eye = eye_ref[...]
    strict = strict_ref[...]
    incl = incl_ref[...]

    def blockdiag(x):
        return _tile4(x.astype(BF16)) * bd

    n_chunks = tc // CHUNK
    n_quads = V // QUAD
    chains = [(c, q) for c in range(n_chunks) for q in range(n_quads)]
    rows = lambda c: slice(c * CHUNK, (c + 1) * CHUNK)
    lanes = lambda q: slice(q * QUAD, (q + 1) * QUAD)
    sl = {cq: (rows(cq[0]), lanes(cq[1])) for cq in chains}

    pb, pk = {}, {}
    for cq in chains:
        x = jnp.concatenate([at[sl[cq]], rt[sl[cq]].astype(BF16)], axis=0)
        pb[cq] = _dot_nt(x, blockdiag(bt[sl[cq]]))
        pk[cq] = _dot_nt(x, blockdiag(kt[sl[cq]]))
    yield
    t, pw, a_rb, av = {}, {}, {}, {}
    for cq in chains:
        a_ab = pb[cq][0:CHUNK] * strict
        a_rb[cq] = (pb[cq][CHUNK:] * incl).astype(BF16)
        a_k = jnp.concatenate([pk[cq][0:CHUNK] * strict, pk[cq][CHUNK:] * incl], axis=0)
        t[cq] = eye + a_ab
        pw[cq] = _dot(a_ab.astype(BF16), blockdiag(a_ab))
        av[cq] = _dot(a_k.astype(BF16), blockdiag(vb[sl[cq]]))
    yield
    for _ in range(4):
        for cq in chains:
            tp = _dot(jnp.concatenate([t[cq], pw[cq]], axis=0).astype(BF16), blockdiag(pw[cq]))
            t[cq] = t[cq] + tp[0:CHUNK]
            pw[cq] = tp[CHUNK:]
        yield
    for cq in chains:
        t[cq] = (t[cq] + _dot(t[cq].astype(BF16), blockdiag(pw[cq]))).astype(BF16)
    yield
    abar, u0 = {}, {}
    for cq in chains:
        abar[cq] = _dot(t[cq], blockdiag(at[sl[cq]]))
        u0[cq] = _dot(t[cq], blockdiag(av[cq][0:CHUNK]))
    yield
    rbar, y0, gm, nm = {}, {}, {}, {}
    for cq in chains:
        rbar[cq] = (rt[sl[cq]] + _dot(a_rb[cq], blockdiag(abar[cq]))).astype(BF16)
        y0[cq] = av[cq][CHUNK:] + _dot(a_rb[cq], blockdiag(u0[cq]))
    yield
    for cq in chains:
        gm[cq] = (_dot_tn(abar[cq].astype(BF16), bt[sl[cq]]) * bdf).astype(BF16)
        nm[cq] = _dot_tn(jnp.concatenate([u0[cq].astype(BF16), vb[sl[cq]]], axis=0),
                         jnp.concatenate([bt[sl[cq]], kt[sl[cq]]], axis=0)) * bdf
    yield
    s = [s_ref[q] for q in range(n_quads)]
    for c in range(n_chunks):
        wc = jnp.exp(cum[(c + 1) * CHUNK - 1:(c + 1) * CHUNK])
        for q in range(n_quads):
            cq = (c, q)
            sb = s[q].astype(BF16)
            ybuf_ref[sl[cq]] = _dot_nt(rbar[cq], sb) + y0[cq]
            s[q] = (s[q] + _dot(sb, gm[cq]) + nm[cq]) * wc[:, lanes(q)]
    for q in range(n_quads):
        s_ref[q] = s[q]
    yield
    y = ybuf_ref[...]
    inv_n = 1.0 / RWKV_HEAD
    mean = _head_sums(y, bd) * inv_n
    bonus = _head_sums(r * k * vp[_RK:_RK + 1], bd) * v
    yield
    yc = y - mean
    var = _head_sums(yc * yc, bd) * inv_n
    yield
    yn = yc * lax.rsqrt(var + GN_EPS) * vp[_GNG:_GNG + 1] + vp[_GNB:_GNB + 1]
    o_ref[...] = (yn + bonus) * g


def _mixer_kernel(*refs, tc, has_vfirst):
    (x0_ref, xn_ref, sh_ref, sc_ref, g_ref, w_ref,
     gkup_ref, gkb_ref, ng_ref, mu_ref, vp_ref, wup_ref, aup_ref, gup_ref) = refs[:14]
    rest = list(refs[14:])
    vup_ref, vf_ref = (rest.pop(0), rest.pop(0)) if has_vfirst else (None, None)
    (tril_ref, bd_ref, bdf_ref, bdv_ref, bdvt_ref, eye_ref, strict_ref, causal_ref) = rest[:8]
    rest = rest[8:]
    og_ref, or_ref = rest[0], rest[1]
    rest = rest[2:]
    vout_ref = None if has_vfirst else rest.pop(0)
    proj_ref, st_ref, obuf_ref, carry_ref, s_ref, ybuf_ref = rest

    t = pl.program_id(1)
    cur = lax.rem(t, 2)

    @pl.when(t == 0)
    def _():
        st_ref[...] = jnp.zeros_like(st_ref)
        carry_ref[...] = jnp.zeros_like(carry_ref)
        s_ref[...] = jnp.zeros_like(s_ref)
        for _ in _proj_stages(x0_ref, sh_ref, sc_ref, g_ref, w_ref, proj_ref.at[0]):
            pass

    p = proj_ref.at[cur]
    _interleave([
        (_rwkv_stages(p, mu_ref, vp_ref, wup_ref, aup_ref, gup_ref, vup_ref, vf_ref, tril_ref,
                      bd_ref, bdf_ref, eye_ref, strict_ref, causal_ref, or_ref, vout_ref,
                      carry_ref, s_ref, ybuf_ref, tc), 1),
        (_proj_stages(xn_ref, sh_ref, sc_ref, g_ref, w_ref, proj_ref.at[1 - cur]), 3),
        (_gla_stages(p, gkup_ref, gkb_ref, ng_ref, tril_ref, bd_ref, bdv_ref, bdvt_ref, causal_ref,
                     og_ref, st_ref, obuf_ref, tc), 1),
    ])


def _mixer(x, mod, g, w_proj, gk_up_pad, gk_b, norm_g, mu, vparams, w_up, a_up, g_up, v_up,
           v_first, consts, tc):
    B, T, D = x.shape
    nt = T // tc
    const = lambda b, t: (0, 0)
    has_vfirst = v_first is not None
    tok = lambda w: pl.BlockSpec((None, tc, w), lambda b, t: (b, t, 0))
    single = pl.Buffered(1)
    in_specs = [
        pl.BlockSpec((None, tc, D), lambda b, t: (b, 0, 0)),
        pl.BlockSpec((None, tc, D), lambda b, t: (b, jnp.minimum(t + 1, nt - 1), 0)),
        pl.BlockSpec((None, None, 1, D), lambda b, t: (0, b, 0, 0)),
        pl.BlockSpec((None, None, 1, D), lambda b, t: (1, b, 0, 0)),
        pl.BlockSpec((1, D), const),
        pl.BlockSpec((D, PROJ_COLS), const, pipeline_mode=single),
        pl.BlockSpec((LANE, GLA_KW), const),
        pl.BlockSpec((1, GLA_KW), const),
        pl.BlockSpec((1, GLA_DV), const),
        pl.BlockSpec((1, RWKV_COLS_PAD), const),
        pl.BlockSpec((8, RWKV_VW), const),
        pl.BlockSpec((LANE, RWKV_VW), const),
        pl.BlockSpec((LANE, RWKV_VW), const),
        pl.BlockSpec((LANE, RWKV_VW), const),
    ]
    args = [x, x, mod, mod, g, w_proj, gk_up_pad, gk_b, norm_g, mu, vparams, w_up, a_up, g_up]
    if has_vfirst:
        in_specs += [pl.BlockSpec((LANE, RWKV_VW), const), tok(RWKV_VW)]
        args += [v_up, v_first]
    in_specs += [
        pl.BlockSpec((TRIL_ROWS, TRIL_ROWS), const),
        pl.BlockSpec((QUAD, QUAD), const),
        pl.BlockSpec((QUAD, QUAD), const),
        pl.BlockSpec((QUAD, GLA_VW), const),
        pl.BlockSpec((GLA_VW, QUAD), const),
        pl.BlockSpec((CHUNK, QUAD), const),
        pl.BlockSpec((CHUNK, QUAD), const),
        pl.BlockSpec((CHUNK, QUAD), const),
    ]
    args += [consts["tril"], consts["bd"], consts["bdf"], consts["bdv"], consts["bdvt"],
             consts["eye"], consts["strict"], consts["causal"]]
    out_sds = jax.ShapeDtypeStruct((B, T, RWKV_VW), F32)
    n_out = 2 if has_vfirst else 3
    res = pl.pallas_call(
        functools.partial(_mixer_kernel, tc=tc, has_vfirst=has_vfirst),
        grid=(B, nt),
        in_specs=in_specs,
        out_specs=[tok(RWKV_VW)] * n_out,
        out_shape=[out_sds] * n_out,
        scratch_shapes=[
            pltpu.VMEM((2, tc, PROJ_COLS), F32),
            pltpu.VMEM((GLA_VW, GLA_KW), F32),
            pltpu.VMEM((tc, GLA_VW), F32),
            pltpu.VMEM((8, RWKV_COLS_PAD), F32),
            pltpu.VMEM((RWKV_VW // QUAD, QUAD, QUAD), F32),
            pltpu.VMEM((tc, RWKV_VW), F32),
        ],
        compiler_params=pltpu.CompilerParams(
            dimension_semantics=("parallel", "arbitrary"), vmem_limit_bytes=VMEM_LIMIT),
        name="mixer",
    )(*args)
    if has_vfirst:
        return res[0], res[1], v_first
    return res[0], res[1], res[2]


def _ffn_kernel(x_ref, og_ref, or_ref, wo_g_ref, wo_r_ref, gtm_ref, shf_ref, scf_ref, gtf_ref,
                nfg_ref, wup_ref, cw_ref, cb_ref, wdn_ref, fg_ref, out_ref, tail_ref, act_ref,
                *, tm, fc, d_ff, final):
    @pl.when(pl.program_id(1) == 0)
    def _():
        tail_ref[...] = jnp.zeros_like(tail_ref)

    mixed = _dot(og_ref[...].astype(BF16), wo_g_ref[...]) + _dot(or_ref[...].astype(BF16), wo_r_ref[...])
    x1 = x_ref[...] + gtm_ref[...] * mixed
    h = _modulated_norm(x1, nfg_ref[...], shf_ref[...], scf_ref[...]).astype(BF16)

    row8 = lax.broadcasted_iota(jnp.int32, (8, 1), 0)

    def shifted(u, tl, k):
        r = pltpu.roll(u, k, 0)
        head = r[0:8]
        for i in range(k):
            head = jnp.where(row8 == i, tl[8 - k + i:8 - k + i + 1], head)
        return jnp.concatenate([head, r[8:]], axis=0)

    def conv(u, cols):
        tl = tail_ref[:, cols]
        tail_ref[:, cols] = u[tm - 8:tm]
        return (cw_ref[0:1, cols] * shifted(u, tl, 2) + cw_ref[1:2, cols] * shifted(u, tl, 1)
                + cw_ref[2:3, cols] * u + cb_ref[:, cols])

    n_fc = d_ff // fc
    gate_cols = lambda j: slice(j * fc, (j + 1) * fc)
    val_cols = lambda j: slice(d_ff + j * fc, d_ff + (j + 1) * fc)
    up = lambda j: (_dot(h, wup_ref[:, gate_cols(j)]), _dot(h, wup_ref[:, val_cols(j)]))

    u_cur = up(0)
    for j in range(n_fc):
        u_nxt = up(j + 1) if j + 1 < n_fc else None
        act = _silu(conv(u_cur[0], gate_cols(j))) * conv(u_cur[1], val_cols(j))
        act_ref[:, gate_cols(j)] = act.astype(BF16)
        u_cur = u_nxt
    x2 = x1 + gtf_ref[...] * _dot(act_ref[...], wdn_ref[...])
    if final:
        ms = jnp.mean(x2 * x2, axis=-1, keepdims=True)
        x2 = x2 * lax.rsqrt(ms + NORM_EPS) * fg_ref[...]
    out_ref[...] = x2


def _ffn(x, o_gla, o_rwkv, wo_g, wo_r, mod, nfg, w_up, conv_w, conv_b, w_down, final_g, final, tm, fc):
    B, T, D = x.shape
    d_ff = w_down.shape[0]
    const = lambda b, t: (0, 0)
    tok = lambda w: pl.BlockSpec((None, tm, w), lambda b, t: (b, t, 0))
    modspec = lambda i: pl.BlockSpec((None, None, 1, D), lambda b, t: (i, b, 0, 0))
    single = pl.Buffered(1)
    return pl.pallas_call(
        functools.partial(_ffn_kernel, tm=tm, fc=fc, d_ff=d_ff, final=final),
        grid=(B, T // tm),
        in_specs=[
            tok(D), tok(GLA_VW), tok(RWKV_VW),
            pl.BlockSpec((GLA_VW, D), const, pipeline_mode=single),
            pl.BlockSpec((RWKV_VW, D), const, pipeline_mode=single),
            modspec(2), modspec(3), modspec(4), modspec(5),
            pl.BlockSpec((1, D), const),
            pl.BlockSpec((D, 2 * d_ff), const, pipeline_mode=single),
            pl.BlockSpec((CONV_W, 2 * d_ff), const),
            pl.BlockSpec((1, 2 * d_ff), const),
            pl.BlockSpec((d_ff, D), const, pipeline_mode=single),
            pl.BlockSpec((1, D), const),
        ],
        out_specs=tok(D),
        out_shape=jax.ShapeDtypeStruct((B, T, D), F32),
        scratch_shapes=[pltpu.VMEM((8, 2 * d_ff), F32), pltpu.VMEM((tm, d_ff), BF16)],
        compiler_params=pltpu.CompilerParams(
            dimension_semantics=("parallel", "arbitrary"), vmem_limit_bytes=VMEM_LIMIT),
        name="out_ffn",
    )(x, o_gla, o_rwkv, wo_g, wo_r, mod, mod, mod, mod, nfg, w_up, conv_w, conv_b, w_down, final_g)


def _masks():
    i = jnp.arange(TRIL_ROWS)
    tril = ((i[:, None] >= i[None, :]) & (i[:, None] // CHUNK == i[None, :] // CHUNK)).astype(BF16)
    hq = jnp.arange(QUAD) // RWKV_HEAD
    bd = (hq[:, None] == hq[None, :]).astype(BF16)
    hv = jnp.arange(GLA_VW) // GLA_DV
    bdv = (hq[:, None] == hv[None, :]).astype(BF16)
    ci = jnp.arange(CHUNK)[:, None]
    cj = (jnp.arange(QUAD) % CHUNK)[None, :]
    return dict(tril=tril, bd=bd, bdf=bd.astype(F32), bdv=bdv, bdvt=bdv.T.astype(F32),
                causal=(ci >= cj).astype(F32), strict=(ci > cj).astype(F32), eye=(ci == cj).astype(F32))


def _pad_rows(w, rows, at):
    out = jnp.zeros((rows, w.shape[1]), w.dtype)
    return out.at[at:at + w.shape[0]].set(w)


def kernel(x, c, ada_w, ada_b, norm_mix_g, w_in, w_in_vres, gla_gk_up, gla_gk_b, gla_norm_g,
           rwkv_mu, rwkv_mu_vres, w_lora_up, w0, a_lora_up, a0, g_lora_up, v_lora_up, v0,
           k_k, k_a, r_k, gn_g, gn_b, w_out, norm_ffn_g, ffn_up, ffn_conv_w, ffn_conv_b,
           ffn_down, final_g):
    B, T, D = x.shape
    L = ada_w.shape[0]
    gla_cols = 2 * GLA_KW + 2 * GLA_VW + GLA_GATE_RANK
    tc = min(512, T)
    tm_ffn = min(512, T)
    fc = 256
    consts = _masks()

    mod_all = _adaln(c, ada_w, ada_b)
    v_first = None
    for l in range(L):
        mod = mod_all[l].reshape(B, N_MOD, 1, D).transpose(1, 0, 2, 3)
        w_r = w_in[l][:, gla_cols:]
        mu = rwkv_mu[l]
        if l > 0:
            w_r = jnp.concatenate([w_r, w_in_vres[l - 1]], axis=1)
            mu = jnp.concatenate([mu, rwkv_mu_vres[l - 1]], axis=0)
        w_proj = jnp.concatenate([
            jnp.pad(w_in[l][:, :gla_cols], ((0, 0), (0, GLA_COLS_PAD - gla_cols))),
            jnp.pad(w_r, ((0, 0), (0, RWKV_COLS_PAD - w_r.shape[1])))], axis=1).astype(BF16)
        mu = jnp.pad(mu, (0, RWKV_COLS_PAD - mu.shape[0])).reshape(1, RWKV_COLS_PAD)

        v0_l = v0[l - 1] if l > 0 else jnp.zeros((RWKV_VW,), F32)
        vparams = jnp.stack([w0[l], a0[l], v0_l, k_k[l], k_a[l], r_k[l].reshape(-1), gn_g[l], gn_b[l]])
        v_up = _pad_rows(v_lora_up[l - 1], LANE, 0).astype(BF16) if l > 0 else None
        o_gla, o_rwkv, v_first = _mixer(
            x, mod, norm_mix_g[l].reshape(1, D), w_proj,
            _pad_rows(gla_gk_up[l], LANE, 0), gla_gk_b[l].reshape(1, GLA_KW),
            gla_norm_g[l].reshape(1, GLA_DV), mu, vparams,
            _pad_rows(w_lora_up[l], LANE, 0).astype(BF16),
            _pad_rows(a_lora_up[l], LANE, DECAY_LORA).astype(BF16),
            g_lora_up[l].astype(BF16), v_up, v_first, consts, tc)

        x = _ffn(x, o_gla, o_rwkv, w_out[l][:GLA_VW].astype(BF16), w_out[l][GLA_VW:].astype(BF16),
                 mod, norm_ffn_g[l].reshape(1, D), ffn_up[l].astype(BF16), ffn_conv_w[l],
                 ffn_conv_b[l].reshape(1, -1), ffn_down[l].astype(BF16), final_g.reshape(1, D),
                 l == L - 1, tm_ffn, fc)
    return x
```
